```python
import math
import jax, jax.numpy as jnp
from jax import lax
import numpy as np

D_MODEL = 1024
BATCH = 4
SEQ = 4096
DEPTH = 2
DEC_BATCH = 128
DEC_SEQ = 1
PAST_LEN = 2048
PAGE_SIZE = 128

ATT_HEADS = 8
HEAD_DIM = 64
ATT_WIDTH = ATT_HEADS * HEAD_DIM
ROT_DIM = HEAD_DIM // 4
ROPE_THETA = 500000.0
MOBA_BLOCK = 256
MOBA_TOPK = 3
MOBA_QCHUNK = 32

SSM_HEADS = 24
SSM_HEAD_DIM = 64
SSM_WIDTH = SSM_HEADS * SSM_HEAD_DIM
SSM_GROUPS = 4
D_STATE = 128
CONV_W = 4
CONV_DIM = SSM_WIDTH + 2 * SSM_GROUPS * D_STATE
SSD_CHUNK = 128

MIX_WIDTH = ATT_WIDTH + SSM_WIDTH
IN_PROJ_DIM = 3 * ATT_WIDTH + SSM_WIDTH + CONV_DIM + SSM_HEADS

N_MEM = 256
X_HEADS = 4
X_HEAD_DIM = D_MODEL // X_HEADS
X_WIDTH = X_HEADS * X_HEAD_DIM

D_FF = ((8 * D_MODEL // 3 + 127) // 128) * 128
EPS = 1e-6

kernel_name = 'hybrid_moba_ssd_macaron_decode_step'


def rms_norm(x, w):
    xf = x.astype(jnp.float32)
    y = xf * lax.rsqrt(jnp.mean(xf * xf, axis=-1, keepdims=True) + EPS)
    return (y * w.astype(jnp.float32)).astype(x.dtype)


def swiglu(x, w_in, w_out):
    gate, up = jnp.split(x @ w_in, 2, axis=-1)
    return (jax.nn.silu(gate) * up) @ w_out


def rope_partial(x, pos):
    half = ROT_DIM // 2
    inv_freq = ROPE_THETA ** (-(jnp.arange(0, ROT_DIM, 2, dtype=jnp.float32) / ROT_DIM))
    ang = pos.astype(jnp.float32)[:, None] * inv_freq[None, :]
    cos = jnp.cos(ang)[:, None, :]
    sin = jnp.sin(ang)[:, None, :]
    xf = x.astype(jnp.float32)
    x1, x2, rest = xf[..., :half], xf[..., half:ROT_DIM], x[..., ROT_DIM:]
    rot = jnp.concatenate([x1 * cos - x2 * sin, x1 * sin + x2 * cos], axis=-1)
    return jnp.concatenate([rot.astype(x.dtype), rest], axis=-1)


def moba_core(q, qpos, kb, vb, kmean):
    nb = kb.shape[0]
    nq = q.shape[0]
    own = qpos // MOBA_BLOCK
    gate = jnp.einsum('qhd,nhd->qhn', q.astype(jnp.float32), kmean)
    past = jnp.arange(nb)[None, None, :] < own[:, None, None]
    gate = jnp.where(past, gate, -jnp.inf)
    _, top = lax.top_k(gate, min(MOBA_TOPK, nb))
    top_ok = top < own[:, None, None]
    own_b = jnp.broadcast_to(own[:, None, None], (nq, ATT_HEADS, 1)).astype(top.dtype)
    sel = jnp.concatenate([top, own_b], axis=-1)
    ok = jnp.concatenate([top_ok, jnp.ones((nq, ATT_HEADS, 1), bool)], axis=-1)
    h_idx = jnp.arange(ATT_HEADS)[None, :, None]
    kg = kb[sel, :, h_idx]
    vg = vb[sel, :, h_idx]
    key_pos = sel[..., None] * MOBA_BLOCK + jnp.arange(MOBA_BLOCK)
    mask = ok[..., None] & (key_pos <= qpos[:, None, None, None])
    s = jnp.einsum('qhd,qhjrd->qhjr', q, kg).astype(jnp.float32) * (HEAD_DIM ** -0.5)
    s = jnp.where(mask, s, -jnp.inf)
    p = jax.nn.softmax(s.reshape(nq, ATT_HEADS, -1), axis=-1).reshape(s.shape)
    return jnp.einsum('qhjr,qhjrd->qhd', p.astype(vg.dtype), vg)


def moba_attention(q, k_all, v_all, qpos):
    b, t = k_all.shape[:2]
    lq = q.shape[1]
    nb = -(-t // MOBA_BLOCK)
    pad = nb * MOBA_BLOCK - t
    widths = ((0, 0), (0, pad), (0, 0), (0, 0))
    kb = jnp.pad(k_all, widths).reshape(b, nb, MOBA_BLOCK, ATT_HEADS, HEAD_DIM)
    vb = jnp.pad(v_all, widths).reshape(b, nb, MOBA_BLOCK, ATT_HEADS, HEAD_DIM)
    kmean = jnp.mean(kb.astype(jnp.float32), axis=2)
    qc = math.gcd(lq, MOBA_QCHUNK)
    nqc = lq // qc
    q_chunks = q.reshape(b, nqc, qc, ATT_HEADS, HEAD_DIM).transpose(1, 0, 2, 3, 4)
    pos_chunks = qpos.reshape(nqc, qc)
    core = jax.vmap(moba_core, in_axes=(0, None, 0, 0, 0))
    out = lax.map(lambda xs: core(xs[0], xs[1], kb, vb, kmean), (q_chunks, pos_chunks))
    return out.transpose(1, 0, 2, 3, 4).reshape(b, lq, ATT_HEADS, HEAD_DIM)


def causal_conv(xbc, prev, w, bias):
    l = xbc.shape[1]
    xp = jnp.concatenate([prev.astype(xbc.dtype), xbc], axis=1)
    out = bias
    for i in range(CONV_W):
        out = out + xp[:, i:i + l] * w[i]
    return jax.nn.silu(out), xp[:, -(CONV_W - 1):]


def ssd_scan(x, dt, a, bm, cm, h0):
    bsz, l, h, p = x.shape
    g, n = bm.shape[-2:]
    r = h // g
    tc = math.gcd(l, SSD_CHUNK)
    c = l // tc
    xdt = (x * dt[..., None]).reshape(bsz, c, tc, g, r, p)
    acum = jnp.cumsum(a.reshape(bsz, c, tc, g, r), axis=2)
    bc = bm.reshape(bsz, c, tc, g, n)
    cc = cm.reshape(bsz, c, tc, g, n)
    causal = jnp.tril(jnp.ones((tc, tc), bool))[None, None, :, :, None, None]
    seg = acum[:, :, :, None] - acum[:, :, None, :]
    decay = jnp.exp(jnp.where(causal, seg, -jnp.inf))
    cb = jnp.einsum('bctgn,bcsgn->bctsg', cc, bc)
    y_diag = jnp.einsum('bctsgr,bcsgrp->bctgrp', cb[..., None] * decay, xdt)
    to_end = jnp.exp(acum[:, :, -1:] - acum)
    chunk_states = jnp.einsum('bcsgn,bcsgrp->bcgrpn', bc, xdt * to_end[..., None])
    chunk_decay = jnp.exp(acum[:, :, -1])

    def step(h_prev, inp):
        dec, st = inp
        return dec[..., None, None] * h_prev + st, h_prev

    h_last, h_enter = lax.scan(step, h0.reshape(bsz, g, r, p, n),
                               (chunk_decay.transpose(1, 0, 2, 3),
                                chunk_states.transpose(1, 0, 2, 3, 4, 5)))
    h_enter = h_enter.transpose(1, 0, 2, 3, 4, 5)
    y_off = jnp.einsum('bctgn,bcgrpn->bctgrp', cc, h_enter) * jnp.exp(acum)[..., None]
    y = (y_diag + y_off).reshape(bsz, l, h, p)
    return y, h_last.reshape(bsz, h, p, n)


def parallel_mixer(u, pos, kv_past, h0, conv_prev, lw):
    b, l, _ = u.shape
    splits = [ATT_WIDTH, 2 * ATT_WIDTH, 3 * ATT_WIDTH, 3 * ATT_WIDTH + SSM_WIDTH,
              3 * ATT_WIDTH + SSM_WIDTH + CONV_DIM]
    q, k, v, z, xbc, dt_raw = jnp.split(u @ lw['w_mix_in'], splits, axis=-1)
    q = rope_partial(q.reshape(b, l, ATT_HEADS, HEAD_DIM), pos)
    k = rope_partial(k.reshape(b, l, ATT_HEADS, HEAD_DIM), pos)
    v = v.reshape(b, l, ATT_HEADS, HEAD_DIM)
    if kv_past is None:
        k_all, v_all = k, v
    else:
        k_all = jnp.concatenate([kv_past[0].astype(k.dtype), k], axis=1)
        v_all = jnp.concatenate([kv_past[1].astype(v.dtype), v], axis=1)
    att = moba_attention(q, k_all, v_all, pos).reshape(b, l, ATT_WIDTH)
    xbc, conv_state = causal_conv(xbc, conv_prev, lw['conv_w'], lw['conv_b'])
    xs, bm, cm = jnp.split(xbc, [SSM_WIDTH, SSM_WIDTH + SSM_GROUPS * D_STATE], axis=-1)
    dt = jax.nn.softplus(dt_raw.astype(jnp.float32) + lw['dt_bias'].astype(jnp.float32))
    a = dt * (-jnp.exp(lw['a_log'].astype(jnp.float32)))
    xh = xs.reshape(b, l, SSM_HEADS, SSM_HEAD_DIM).astype(jnp.float32)
    y, h_last = ssd_scan(xh, dt, a,
                         bm.reshape(b, l, SSM_GROUPS, D_STATE).astype(jnp.float32),
                         cm.reshape(b, l, SSM_GROUPS, D_STATE).astype(jnp.float32),
                         h0.astype(jnp.float32))
    y = y + xh * lw['d_skip'].astype(jnp.float32)[None, None, :, None]
    y = y.reshape(b, l, SSM_WIDTH) * jax.nn.silu(z.astype(jnp.float32))
    yg = y.reshape(b, l, SSM_GROUPS, -1)
    yg = yg * lax.rsqrt(jnp.mean(yg * yg, axis=-1, keepdims=True) + EPS)
    y = (yg.reshape(b, l, SSM_WIDTH) * lw['norm_ssm_out'].astype(jnp.float32)).astype(u.dtype)
    mixed = jnp.concatenate([att, y], axis=-1) @ lw['w_mix_out']
    return mixed, k, v, h_last.astype(h0.dtype), conv_state


def cross_attend(u, mem_k, mem_v, w_q, w_o):
    b, l, _ = u.shape
    q = (u @ w_q).reshape(b, l, X_HEADS, X_HEAD_DIM)
    s = jnp.einsum('blhd,bmhd->bhlm', q, mem_k).astype(jnp.float32) * (X_HEAD_DIM ** -0.5)
    p = jax.nn.softmax(s, axis=-1).astype(mem_v.dtype)
    o = jnp.einsum('bhlm,bmhd->blhd', p, mem_v).reshape(b, l, X_WIDTH)
    return o @ w_o


def layer_forward(x, pos, kv_past, h0, conv_prev, mem_k, mem_v, lw):
    x = x + 0.5 * swiglu(rms_norm(x, lw['norm_ffn1']), lw['ffn1_w_in'], lw['ffn1_w_out'])
    mixed, k, v, h_last, conv_state = parallel_mixer(rms_norm(x, lw['norm_mix']), pos, kv_past,
                                                     h0, conv_prev, lw)
    x = x + mixed
    x = x + cross_attend(rms_norm(x, lw['norm_xattn']), mem_k, mem_v, lw['w_xq'], lw['w_xo'])
    x = x + 0.5 * swiglu(rms_norm(x, lw['norm_ffn2']), lw['ffn2_w_in'], lw['ffn2_w_out'])
    return x, k, v, h_last, conv_state


def setup_inputs(seed: int = 0) -> dict:
    key = jax.random.key(seed)
    ks = jax.random.split(key, 40)
    n_pages = PAST_LEN // PAGE_SIZE
    n_used = DEC_BATCH * n_pages
    n_phys = n_used + max(1, n_used // 4)

    def nrm(k, shape, scale):
        return jax.random.normal(k, shape, jnp.float32) * scale

    def gain(k, shape):
        return 1.0 + 0.05 * jax.random.normal(k, shape, jnp.float32)

    dt0 = jnp.exp(jax.random.uniform(ks[20], (DEPTH, SSM_HEADS), jnp.float32,
                                     math.log(1e-3), math.log(1e-1)))
    return {
        'x_prompt': nrm(ks[0], (BATCH, SEQ, D_MODEL), 1.0),
        'x_sample': nrm(ks[1], (DEC_BATCH, DEC_SEQ, D_MODEL), 1.0),
        'mem_prompt': nrm(ks[2], (BATCH, N_MEM, D_MODEL), 1.0),
        'cache_k': nrm(ks[3], (DEPTH, n_phys, PAGE_SIZE, ATT_HEADS, HEAD_DIM), 1.0),
        'cache_v': nrm(ks[4], (DEPTH, n_phys, PAGE_SIZE, ATT_HEADS, HEAD_DIM), 1.0),
        'page_table': jax.random.permutation(ks[5], n_phys)[:n_used]
                      .reshape(DEC_BATCH, n_pages).astype(jnp.int32),
        'state_ssm': nrm(ks[6], (DEPTH, DEC_BATCH, SSM_HEADS, SSM_HEAD_DIM, D_STATE), 0.5),
        'state_conv': nrm(ks[7], (DEPTH, DEC_BATCH, CONV_W - 1, CONV_DIM), 1.0),
        'cache_mem_k': nrm(ks[8], (DEPTH, DEC_BATCH, N_MEM, X_HEADS, X_HEAD_DIM), 1.0),
        'cache_mem_v': nrm(ks[9], (DEPTH, DEC_BATCH, N_MEM, X_HEADS, X_HEAD_DIM), 1.0),
        'norm_ffn1': gain(ks[10], (DEPTH, D_MODEL)),
        'ffn1_w_in': nrm(ks[11], (DEPTH, D_MODEL, 2 * D_FF), D_MODEL ** -0.5),
        'ffn1_w_out': nrm(ks[12], (DEPTH, D_FF, D_MODEL), D_FF ** -0.5),
        'norm_mix': gain(ks[13], (DEPTH, D_MODEL)),
        'w_mix_in': nrm(ks[14], (DEPTH, D_MODEL, IN_PROJ_DIM), D_MODEL ** -0.5),
        'conv_w': nrm(ks[15], (DEPTH, CONV_W, CONV_DIM), CONV_W ** -0.5),
        'conv_b': nrm(ks[16], (DEPTH, CONV_DIM), 0.02),
        'dt_bias': dt0 + jnp.log(-jnp.expm1(-dt0)),
        'a_log': jnp.log(jax.random.uniform(ks[17], (DEPTH, SSM_HEADS), jnp.float32, 1.0, 16.0)),
        'd_skip': 1.0 + 0.1 * jax.random.normal(ks[18], (DEPTH, SSM_HEADS), jnp.float32),
        'norm_ssm_out': gain(ks[19], (DEPTH, SSM_WIDTH)),
        'w_mix_out': nrm(ks[21], (DEPTH, MIX_WIDTH, D_MODEL), MIX_WIDTH ** -0.5),
        'norm_xattn': gain(ks[22], (DEPTH, D_MODEL)),
        'w_xq': nrm(ks[23], (DEPTH, D_MODEL, X_WIDTH), D_MODEL ** -0.5),
        'w_xk': nrm(ks[24], (DEPTH, D_MODEL, X_WIDTH), D_MODEL ** -0.5),
        'w_xv': nrm(ks[25], (DEPTH, D_MODEL, X_WIDTH), D_MODEL ** -0.5),
        'w_xo': nrm(ks[26], (DEPTH, X_WIDTH, D_MODEL), X_WIDTH ** -0.5),
        'norm_ffn2': gain(ks[27], (DEPTH, D_MODEL)),
        'ffn2_w_in': nrm(ks[28], (DEPTH, D_MODEL, 2 * D_FF), D_MODEL ** -0.5),
        'ffn2_w_out': nrm(ks[29], (DEPTH, D_FF, D_MODEL), D_FF ** -0.5),
        'norm_final': gain(ks[30], (D_MODEL,)),
    }


def reference(x_prompt, x_sample, mem_prompt, cache_k, cache_v, page_table, state_ssm,
              state_conv, cache_mem_k, cache_mem_v, norm_ffn1, ffn1_w_in, ffn1_w_out,
              norm_mix, w_mix_in, conv_w, conv_b, dt_bias, a_log, d_skip, norm_ssm_out,
              w_mix_out, norm_xattn, w_xq, w_xk, w_xv, w_xo, norm_ffn2, ffn2_w_in,
              ffn2_w_out, norm_final):
    b_p, s_p = x_prompt.shape[:2]
    b_s, s_s = x_sample.shape[:2]
    n_mem = mem_prompt.shape[1]
    past_len = page_table.shape[1] * cache_k.shape[2]
    pos_p = jnp.arange(s_p, dtype=jnp.int32)
    pos_s = past_len + jnp.arange(s_s, dtype=jnp.int32)
    xp, xs = x_prompt, x_sample
    k_p_l, v_p_l, k_s_l, v_s_l = [], [], [], []
    h_p_l, h_s_l, c_p_l, c_s_l, mk_l, mv_l = [], [], [], [], [], []
    for l in range(DEPTH):
        lw = {'norm_ffn1': norm_ffn1[l], 'ffn1_w_in': ffn1_w_in[l], 'ffn1_w_out': ffn1_w_out[l],
              'norm_mix': norm_mix[l], 'w_mix_in': w_mix_in[l], 'conv_w': conv_w[l],
              'conv_b': conv_b[l], 'dt_bias': dt_bias[l], 'a_log': a_log[l], 'd_skip': d_skip[l],
              'norm_ssm_out': norm_ssm_out[l], 'w_mix_out': w_mix_out[l],
              'norm_xattn': norm_xattn[l], 'w_xq': w_xq[l], 'w_xo': w_xo[l],
              'norm_ffn2': norm_ffn2[l], 'ffn2_w_in': ffn2_w_in[l], 'ffn2_w_out': ffn2_w_out[l]}
        mem_k_p = (mem_prompt @ w_xk[l]).reshape(b_p, n_mem, X_HEADS, X_HEAD_DIM)
        mem_v_p = (mem_prompt @ w_xv[l]).reshape(b_p, n_mem, X_HEADS, X_HEAD_DIM)
        h0_p = jnp.zeros((b_p, SSM_HEADS, SSM_HEAD_DIM, D_STATE), x_prompt.dtype)
        conv0_p = jnp.zeros((b_p, CONV_W - 1, CONV_DIM), x_prompt.dtype)
        xp, k_p, v_p, h_p, c_p = layer_forward(xp, pos_p, None, h0_p, conv0_p,
                                               mem_k_p, mem_v_p, lw)
        k_past = cache_k[l][page_table].reshape(b_s, past_len, ATT_HEADS, HEAD_DIM)
        v_past = cache_v[l][page_table].reshape(b_s, past_len, ATT_HEADS, HEAD_DIM)
        xs, k_s, v_s, h_s, c_s = layer_forward(xs, pos_s, (k_past, v_past), state_ssm[l],
                                               state_conv[l], cache_mem_k[l], cache_mem_v[l], lw)
        k_p_l.append(k_p); v_p_l.append(v_p); k_s_l.append(k_s); v_s_l.append(v_s)
        h_p_l.append(h_p); h_s_l.append(h_s); c_p_l.append(c_p); c_s_l.append(c_s)
        mk_l.append(mem_k_p); mv_l.append(mem_v_p)
    y_prompt = rms_norm(xp, norm_final)
    y_sample = rms_norm(xs, norm_final)
    new_k_prompt = jnp.stack(k_p_l)
    new_v_prompt = jnp.stack(v_p_l)
    new_k_sample = jnp.stack(k_s_l)
    new_v_sample = jnp.stack(v_s_l)
    new_ssm_prompt = jnp.stack(h_p_l)
    new_ssm_sample = jnp.stack(h_s_l)
    new_conv_prompt = jnp.stack(c_p_l)
    new_conv_sample = jnp.stack(c_s_l)
    new_mem_k_prompt = jnp.stack(mk_l)
    new_mem_v_prompt = jnp.stack(mv_l)
    return (y_prompt, y_sample, new_k_prompt, new_v_prompt, new_k_sample, new_v_sample,
            new_ssm_prompt, new_ssm_sample, new_conv_prompt, new_conv_sample,
            new_mem_k_prompt, new_mem_v_prompt)
```

```python
import functools
import math

import jax
import jax.numpy as jnp
from jax import lax
from jax.experimental import pallas as pl
from jax.experimental.pallas import tpu as pltpu

F32 = jnp.float32
BF16 = jnp.bfloat16

D_MODEL = 1024
ATT_HEADS = 8
HEAD_DIM = 64
ATT_WIDTH = ATT_HEADS * HEAD_DIM
ROT_DIM = HEAD_DIM // 4
ROPE_THETA = 500000.0
MOBA_BLOCK = 256
MOBA_TOPK = 3
SSM_HEADS = 24
SSM_HEAD_DIM = 64
SSM_WIDTH = SSM_HEADS * SSM_HEAD_DIM
SSM_GROUPS = 4
HEADS_PER_GROUP = SSM_HEADS // SSM_GROUPS
D_STATE = 128
CONV_W = 4
BC_WIDTH = SSM_GROUPS * D_STATE
CONV_DIM = SSM_WIDTH + 2 * BC_WIDTH
SSD_CHUNK = 128
MIX_WIDTH = ATT_WIDTH + SSM_WIDTH
X_HEADS = 4
X_HEAD_DIM = D_MODEL // X_HEADS
D_FF = 2816
EPS = 1e-6

LANES = 128
SUBLANES = 8
VMEM_LIMIT = 56 * 1024 * 1024

DT_PAD = LANES
IN_PROJ_PAD = 3 * ATT_WIDTH + SSM_WIDTH + CONV_DIM + DT_PAD
NEG_BIG = -1e30
HIGHEST = lax.Precision.HIGHEST

_NT = (((1,), (1,)), ((), ()))
_TN = (((0,), (0,)), ((), ()))


def _params(*sem):
    return pltpu.CompilerParams(dimension_semantics=sem, vmem_limit_bytes=VMEM_LIMIT)


def _rms(x, g):
    return x * lax.rsqrt(jnp.mean(x * x, axis=-1, keepdims=True) + EPS) * g


def _silu(x):
    return x * jax.nn.sigmoid(x)


def _softplus(x):
    return jnp.maximum(x, 0.0) + jnp.log1p(jnp.exp(-jnp.abs(x)))


def _ffn_body(x_ref, g_ref, wg_ref, wu_ref, wo_ref, gf_ref, o_ref, u_sc, acc_sc, *, final_norm):
    j = pl.program_id(1)

    @pl.when(j == 0)
    def _():
        u_sc[...] = _rms(x_ref[...], g_ref[...]).astype(BF16)
        acc_sc[...] = jnp.zeros_like(acc_sc)

    u = u_sc[...]
    gate = jnp.dot(u, wg_ref[...], preferred_element_type=F32)
    up = jnp.dot(u, wu_ref[...], preferred_element_type=F32)
    h = (_silu(gate) * up).astype(BF16)
    acc_sc[...] += jnp.dot(h, wo_ref[...], preferred_element_type=F32)

    @pl.when(j == pl.num_programs(1) - 1)
    def _():
        y = x_ref[...] + 0.5 * acc_sc[...]
        if final_norm:
            y = _rms(y, gf_ref[...])
        o_ref[...] = y


def _ffn(x, g, w_in, w_out, g_final, final_norm):
    n = x.shape[0]
    tm = min(512, n)
    nf = 2
    tf = D_FF // nf
    return pl.pallas_call(
        functools.partial(_ffn_body, final_norm=final_norm),
        grid=(n // tm, nf),
        in_specs=[
            pl.BlockSpec((tm, D_MODEL), lambda i, j: (i, 0)),
            pl.BlockSpec((1, D_MODEL), lambda i, j: (0, 0)),
            pl.BlockSpec((D_MODEL, tf), lambda i, j: (0, j)),
            pl.BlockSpec((D_MODEL, tf), lambda i, j: (0, j + nf)),
            pl.BlockSpec((tf, D_MODEL), lambda i, j: (j, 0)),
            pl.BlockSpec((1, D_MODEL), lambda i, j: (0, 0)),
        ],
        out_specs=pl.BlockSpec((tm, D_MODEL), lambda i, j: (i, 0)),
        out_shape=jax.ShapeDtypeStruct((n, D_MODEL), F32),
        scratch_shapes=[pltpu.VMEM((tm, D_MODEL), BF16), pltpu.VMEM((tm, D_MODEL), F32)],
        compiler_params=_params("parallel", "arbitrary"),
        name="ffn",
    )(x, g, w_in, w_in, w_out, g_final)


def _rope(x, cos, sin_hi, sin_lo):
    outs = []
    for c in range(ATT_WIDTH // LANES):
        xc = x[:, c * LANES:(c + 1) * LANES]
        from_hi = pltpu.roll(xc, LANES - ROT_DIM // 2, 1)
        from_lo = pltpu.roll(xc, ROT_DIM // 2, 1)
        outs.append(xc * cos + from_hi * sin_hi + from_lo * sin_lo)
    return jnp.concatenate(outs, axis=1)


def _inproj_body(x_ref, g_ref, w_ref, cos_ref, shi_ref, slo_ref,
                 q_ref, k_ref, v_ref, kb_ref, vb_ref, z_ref, xbc_ref, dt_ref, ksum_ref):
    u = _rms(x_ref[...], g_ref[...]).astype(BF16)

    def proj(lo, hi):
        return jnp.dot(u, w_ref[:, lo:hi], preferred_element_type=F32)

    cos, shi, slo = cos_ref[...], shi_ref[...], slo_ref[...]
    a = ATT_WIDTH
    q_ref[...] = _rope(proj(0, a), cos, shi, slo)
    k = _rope(proj(a, 2 * a), cos, shi, slo)
    k_ref[...] = k
    kb_ref[...] = k.astype(BF16)
    ksum_ref[0] = jnp.sum(k, axis=0, keepdims=True)
    v = proj(2 * a, 3 * a)
    v_ref[...] = v
    vb_ref[...] = v.astype(BF16)
    o = 3 * a
    z_ref[...] = proj(o, o + SSM_WIDTH)
    o += SSM_WIDTH
    xbc_ref[...] = proj(o, o + CONV_DIM)
    o += CONV_DIM
    dt_ref[...] = proj(o, o + DT_PAD)


def _inproj(x, g, w, cos, shi, slo, tm):
    n = x.shape[0]
    nt = n // tm
    ntab = cos.shape[0] // tm
    row = lambda width: pl.BlockSpec((tm, width), lambda i: (i, 0))
    tab = pl.BlockSpec((tm, LANES), lambda i: (i % ntab, 0))
    sds = lambda width, dt: jax.ShapeDtypeStruct((n, width), dt)
    return pl.pallas_call(
        _inproj_body,
        grid=(nt,),
        in_specs=[row(D_MODEL), pl.BlockSpec((1, D_MODEL), lambda i: (0, 0)),
                  pl.BlockSpec((D_MODEL, IN_PROJ_PAD), lambda i: (0, 0)), tab, tab, tab],
        out_specs=[row(ATT_WIDTH)] * 5 + [row(SSM_WIDTH), row(CONV_DIM), row(DT_PAD),
                                           pl.BlockSpec((1, 1, ATT_WIDTH), lambda i: (i, 0, 0))],
        out_shape=[sds(ATT_WIDTH, F32), sds(ATT_WIDTH, F32), sds(ATT_WIDTH, F32),
                   sds(ATT_WIDTH, BF16), sds(ATT_WIDTH, BF16),
                   sds(SSM_WIDTH, F32), sds(CONV_DIM, F32), sds(DT_PAD, F32),
                   jax.ShapeDtypeStruct((nt, 1, ATT_WIDTH), F32)],
        compiler_params=_params("parallel"),
        name="inproj",
    )(x, g, w, cos, shi, slo)


def _moba_prompt_body(q_ref, k_ref, v_ref, kmean_ref, o_ref, *, nb):
    i = pl.program_id(1)
    bs = MOBA_BLOCK
    lane = lax.broadcasted_iota(jnp.int32, (1, LANES), 1)
    lo_f = (lane < HEAD_DIM)
    qi = lax.broadcasted_iota(jnp.int32, (bs, bs), 0)
    ki = lax.broadcasted_iota(jnp.int32, (bs, bs), 1)
    causal_bias = jnp.where(ki <= qi, 0.0, NEG_BIG).astype(F32)
    eye = (qi == ki).astype(BF16)
    blk_row = lax.broadcasted_iota(jnp.int32, (nb, bs), 0)
    own_start = pl.multiple_of(i * bs, bs)

    for pair in range(ATT_HEADS // 2):
        cols = slice(pair * LANES, (pair + 1) * LANES)
        q_pair = q_ref[:, cols]
        km_pair = kmean_ref[0, :, cols]
        q_m = [jnp.where(lo_f, q_pair, 0.0), jnp.where(lo_f, 0.0, q_pair)]
        q_b = [(qm * (HEAD_DIM ** -0.5)).astype(BF16) for qm in q_m]

        sel_bias = []
        for e in range(2):
            gt = lax.dot_general(km_pair, q_m[e], _NT, precision=HIGHEST,
                                 preferred_element_type=F32)
            cnt = jnp.zeros((nb, bs), F32)
            for m in range(nb):
                row = gt[m:m + 1, :]
                beats = (row > gt) | ((row == gt) & (m < blk_row))
                cnt = cnt + jnp.where(beats, 1.0, 0.0) * jnp.where(m < i, 1.0, 0.0)
            chosen = (cnt < MOBA_TOPK) & (blk_row < i)
            bias_t = jnp.where(chosen, 0.0, NEG_BIG)
            if nb < LANES:
                bias_t = jnp.concatenate(
                    [bias_t, jnp.full((LANES - nb, bs), NEG_BIG, F32)], axis=0)
            sel_bias.append(lax.dot_general(eye, bias_t.astype(BF16), _NT,
                                            preferred_element_type=F32).astype(BF16))

        def attend(k_blk, v_blk, bias, state):
            m0, l0, m1, l1, acc = state
            v_e = [jnp.where(lo_f, v_blk, 0).astype(BF16), jnp.where(lo_f, 0, v_blk).astype(BF16)]
            new, alphas, pv = [], [], None
            for e, (m_prev, l_prev) in enumerate(((m0, l0), (m1, l1))):
                s = lax.dot_general(q_b[e], k_blk, _NT, preferred_element_type=F32) + bias[e]
                m_new = jnp.maximum(m_prev, jnp.max(s, axis=1, keepdims=True))
                alpha = jnp.exp(m_prev - m_new)
                p = jnp.exp(s - m_new)
                l_new = alpha * l_prev + jnp.sum(p, axis=1, keepdims=True)
                contrib = jnp.dot(p.astype(BF16), v_e[e], preferred_element_type=F32)
                pv = contrib if pv is None else pv + contrib
                new += [m_new, l_new]
                alphas.append(alpha)
            acc = acc * jnp.where(lo_f, alphas[0], alphas[1]) + pv
            return (new[0], new[1], new[2], new[3], acc)

        init = (jnp.full((bs, 1), NEG_BIG, F32), jnp.zeros((bs, 1), F32),
                jnp.full((bs, 1), NEG_BIG, F32), jnp.zeros((bs, 1), F32),
                jnp.zeros((bs, LANES), F32))
        state = attend(k_ref[pl.ds(own_start, bs), cols], v_ref[pl.ds(own_start, bs), cols],
                       (causal_bias, causal_bias), init)

        def past(n, st):
            start = pl.multiple_of(n * bs, bs)
            onehot = (lax.broadcasted_iota(jnp.int32, (LANES, bs), 0) == n).astype(BF16)
            bias = [jnp.dot(sb, onehot, preferred_element_type=F32) for sb in sel_bias]
            return attend(k_ref[pl.ds(start, bs), cols], v_ref[pl.ds(start, bs), cols], bias, st)

        m0, l0, m1, l1, acc = lax.fori_loop(0, i, past, state)
        o_ref[:, cols] = (acc * jnp.where(lo_f, 1.0 / l0, 1.0 / l1)).astype(o_ref.dtype)


def _moba_prompt(q, kb, vb, kmean, batch, seq):
    nb = seq // MOBA_BLOCK
    return pl.pallas_call(
        functools.partial(_moba_prompt_body, nb=nb),
        grid=(batch, nb),
        in_specs=[
            pl.BlockSpec((MOBA_BLOCK, ATT_WIDTH), lambda b, i: (b * nb + i, 0)),
            pl.BlockSpec((seq, ATT_WIDTH), lambda b, i: (b, 0)),
            pl.BlockSpec((seq, ATT_WIDTH), lambda b, i: (b, 0)),
            pl.BlockSpec((1, nb, ATT_WIDTH), lambda b, i: (b, 0, 0)),
        ],
        out_specs=pl.BlockSpec((MOBA_BLOCK, ATT_WIDTH), lambda b, i: (b * nb + i, 0)),
        out_shape=jax.ShapeDtypeStruct((batch * seq, ATT_WIDTH), BF16),
        compiler_params=_params("parallel", "arbitrary"),
        name="moba_prompt",
    )(q, kb, vb, kmean)


def _moba_sample_body(pt_ref, q_ref, kn_ref, vn_ref, k0_ref, k1_ref, v0_ref, v1_ref, o_ref,
                      m_sc, l_sc, g_sc, acc_sc, *, nb):
    del pt_ref
    n = pl.program_id(1)
    lane = lax.broadcasted_iota(jnp.int32, (ATT_HEADS, ATT_WIDTH), 1)
    head = lax.broadcasted_iota(jnp.int32, (ATT_HEADS, ATT_WIDTH), 0)
    own_lanes = (lane // HEAD_DIM) == head
    q_rows = jnp.where(own_lanes, q_ref[0], 0.0)
    q_scaled = (q_rows * (HEAD_DIM ** -0.5)).astype(BF16)

    k_blk = jnp.concatenate([k0_ref[0], k1_ref[0]], axis=0)
    v_blk = jnp.concatenate([v0_ref[0], v1_ref[0]], axis=0)
    kmean = jnp.mean(k_blk, axis=0, keepdims=True)
    g_sc[n] = jnp.sum(q_rows * kmean, axis=1, keepdims=True)
    s = lax.dot_general(q_scaled, k_blk.astype(BF16), _NT, preferred_element_type=F32)
    m = jnp.max(s, axis=1, keepdims=True)
    p = jnp.exp(s - m)
    m_sc[n] = m
    l_sc[n] = jnp.sum(p, axis=1, keepdims=True)
    acc_sc[n] = jnp.dot(p.astype(BF16), v_blk.astype(BF16), preferred_element_type=F32)

    @pl.when(n == nb - 1)
    def _():
        gates = [g_sc[b] for b in range(nb)]
        s_own = jnp.sum(q_rows * (HEAD_DIM ** -0.5) * kn_ref[0], axis=1, keepdims=True)
        m_tot = s_own
        chosen = []
        for b in range(nb):
            cnt = jnp.zeros((ATT_HEADS, 1), F32)
            for c in range(nb):
                if c == b:
                    continue
                beats = (gates[c] > gates[b]) | ((gates[c] == gates[b]) & (c < b))
                cnt = cnt + jnp.where(beats, 1.0, 0.0)
            chosen.append(cnt < MOBA_TOPK)
            m_tot = jnp.where(chosen[b], jnp.maximum(m_tot, m_sc[b]), m_tot)
        w_own = jnp.exp(s_own - m_tot)
        l_tot = w_own
        acc = w_own * vn_ref[0]
        for b in range(nb):
            w = jnp.where(chosen[b], jnp.exp(m_sc[b] - m_tot), 0.0)
            l_tot = l_tot + w * l_sc[b]
            acc = acc + w * acc_sc[b]
        out = jnp.where(own_lanes, acc / l_tot, 0.0)
        o_ref[0] = jnp.sum(out, axis=0, keepdims=True).astype(o_ref.dtype)


def _moba_sample(page_table, q, k_new, v_new, cache_k, cache_v):
    nseq, n_pages = page_table.shape
    page = cache_k.shape[1]
    assert MOBA_BLOCK == 2 * page and n_pages % 2 == 0
    nb = n_pages // 2
    row = pl.BlockSpec((1, 1, ATT_WIDTH), lambda b, n, pt: (b, 0, 0))
    pg = lambda off: pl.BlockSpec((1, page, ATT_WIDTH), lambda b, n, pt: (pt[b, 2 * n + off], 0, 0))
    small = lambda width: pltpu.VMEM((nb, ATT_HEADS, width), F32)
    grid_spec = pltpu.PrefetchScalarGridSpec(
        num_scalar_prefetch=1,
        grid=(nseq, nb),
        in_specs=[row, row, row, pg(0), pg(1), pg(0), pg(1)],
        out_specs=row,
        scratch_shapes=[small(1), small(1), small(1), small(ATT_WIDTH)],
    )
    return pl.pallas_call(
        functools.partial(_moba_sample_body, nb=nb),
        grid_spec=grid_spec,
        out_shape=jax.ShapeDtypeStruct((nseq, 1, ATT_WIDTH), BF16),
        compiler_params=_params("parallel", "arbitrary"),
        name="moba_sample",
    )(page_table, q.reshape(nseq, 1, ATT_WIDTH), k_new.reshape(nseq, 1, ATT_WIDTH),
      v_new.reshape(nseq, 1, ATT_WIDTH), cache_k, cache_k, cache_v, cache_v)


def _gated_norm(y, z, g):
    y = y * _silu(z)
    gw = SSM_WIDTH // SSM_GROUPS
    outs = []
    for grp in range(SSM_GROUPS):
        yg = y[:, grp * gw:(grp + 1) * gw]
        outs.append(yg * lax.rsqrt(jnp.mean(yg * yg, axis=-1, keepdims=True) + EPS))
    return jnp.concatenate(outs, axis=1) * g


def _ssd_prompt_body(xbc_ref, z_ref, dt_ref, cw_ref, cb_ref, dtb_ref, alog_ref, dskip_ref,
                     gn_ref, e64_ref, e128_ref, y_ref, hout_ref, xp_sc, h_sc):
    c = pl.program_id(1)
    t = SSD_CHUNK
    pad = SUBLANES

    @pl.when(c == 0)
    def _():
        xp_sc[0:pad, :] = jnp.zeros((pad, CONV_DIM), F32)
        h_sc[...] = jnp.zeros_like(h_sc)

    xp_sc[pad:pad + t, :] = xbc_ref[...]
    conv = cb_ref[...]
    for w in range(CONV_W):
        conv = conv + xp_sc[pad - (CONV_W - 1) + w:pad - (CONV_W - 1) + w + t, :] * cw_ref[w:w + 1, :]
    xp_sc[pad - (CONV_W - 1):pad, :] = xp_sc[pad + t - (CONV_W - 1):pad + t, :]
    xbc = _silu(conv)
    xs = xbc[:, :SSM_WIDTH]
    bmat = xbc[:, SSM_WIDTH:SSM_WIDTH + BC_WIDTH]
    cmat = xbc[:, SSM_WIDTH + BC_WIDTH:]

    dt = _softplus(dt_ref[...] + dtb_ref[...])
    a = dt * (-jnp.exp(alog_ref[...]))
    ti = lax.broadcasted_iota(jnp.int32, (t, t), 0)
    si = lax.broadcasted_iota(jnp.int32, (t, t), 1)
    causal = si <= ti
    tril = causal.astype(F32)
    acum = jnp.dot(tril, a, precision=HIGHEST, preferred_element_type=F32)
    acum_t = acum.T
    dt_e = jnp.dot(dt, e64_ref[...], precision=HIGHEST, preferred_element_type=F32)
    ac_e = jnp.dot(acum, e64_ref[...], precision=HIGHEST, preferred_element_type=F32)
    ac_w = jnp.dot(acum, e128_ref[...], precision=HIGHEST, preferred_element_type=F32)
    xdt = xs * dt_e
    exp_ac = jnp.exp(ac_e)
    ac_last = ac_e[t - 1:t, :]
    xw = (xdt * jnp.exp(ac_last - ac_e)).astype(BF16)
    chunk_decay = jnp.exp(ac_w[t - 1:t, :])
    xdt_b = xdt.astype(BF16)

    ys = []
    for grp in range(SSM_GROUPS):
        b_g = bmat[:, grp * D_STATE:(grp + 1) * D_STATE].astype(BF16)
        c_g = cmat[:, grp * D_STATE:(grp + 1) * D_STATE].astype(BF16)
        cb = lax.dot_general(c_g, b_g, _NT, preferred_element_type=F32)
        for r in range(HEADS_PER_GROUP):
            h = grp * HEADS_PER_GROUP + r
            hs = slice(h * SSM_HEAD_DIM, (h + 1) * SSM_HEAD_DIM)
            seg = ac_w[:, h * LANES:(h + 1) * LANES] - acum_t[h:h + 1, :]
            lmat = (cb * jnp.exp(jnp.where(causal, seg, -jnp.inf))).astype(BF16)
            y_diag = jnp.dot(lmat, xdt_b[:, hs], preferred_element_type=F32)
            h_prev = h_sc[h]
            y_off = lax.dot_general(c_g, h_prev.astype(BF16), _NT,
                                    preferred_element_type=F32) * exp_ac[:, hs]
            ys.append(y_diag + y_off)
            st = lax.dot_general(xw[:, hs], b_g, _TN, preferred_element_type=F32)
            h_sc[h] = chunk_decay[:, h * LANES:(h + 1) * LANES] * h_prev + st
    y = jnp.concatenate(ys, axis=1) + xs * dskip_ref[...]
    y_ref[...] = _gated_norm(y, z_ref[...], gn_ref[...]).astype(y_ref.dtype)

    @pl.when(c == pl.num_programs(1) - 1)
    def _():
        hout_ref[0] = h_sc[...]


def _expanders():
    heads = jnp.arange(LANES)[:, None]
    e64 = (heads == (jnp.arange(SSM_WIDTH)[None, :] // SSM_HEAD_DIM)).astype(F32)
    e128 = (heads == (jnp.arange(SSM_HEADS * LANES)[None, :] // LANES)).astype(F32)
    return e64, e128


def _pad_heads(v):
    return jnp.pad(v, (0, LANES - SSM_HEADS)).reshape(1, LANES)


def _ssd_prompt(xbc, z, dt, lw, batch, seq):
    nc = seq // SSD_CHUNK
    t = SSD_CHUNK
    e64, e128 = _expanders()
    row = lambda width: pl.BlockSpec((t, width), lambda b, c: (b * nc + c, 0))
    const = lambda shape: pl.BlockSpec(shape, lambda b, c: (0,) * len(shape))
    return pl.pallas_call(
        _ssd_prompt_body,
        grid=(batch, nc),
        in_specs=[row(CONV_DIM), row(SSM_WIDTH), row(DT_PAD),
                  const((CONV_W, CONV_DIM)), const((1, CONV_DIM)), const((1, LANES)),
                  const((1, LANES)), const((1, SSM_WIDTH)), const((1, SSM_WIDTH)),
                  const((LANES, SSM_WIDTH)), const((LANES, SSM_HEADS * LANES))],
        out_specs=[row(SSM_WIDTH),
                   pl.BlockSpec((1, SSM_HEADS, SSM_HEAD_DIM, D_STATE), lambda b, c: (b, 0, 0, 0))],
        out_shape=[jax.ShapeDtypeStruct((batch * seq, SSM_WIDTH), BF16),
                   jax.ShapeDtypeStruct((batch, SSM_HEADS, SSM_HEAD_DIM, D_STATE), F32)],
        scratch_shapes=[pltpu.VMEM((SUBLANES + t, CONV_DIM), F32),
                        pltpu.VMEM((SSM_HEADS, SSM_HEAD_DIM, D_STATE), F32)],
        compiler_params=_params("parallel", "arbitrary"),
        name="ssd_prompt",
    )(xbc, z, dt, lw["conv_w"], lw["conv_b"], lw["dt_bias"], lw["a_log"], lw["d_skip_e"],
      lw["norm_ssm_out"], e64, e128)


def _ssd_sample_pre_body(xbc_ref, cs_ref, dt_ref, cw_ref, cb_ref, dtb_ref, alog_ref, e64_ref,
                         xs_ref, xdt_t_ref, dec_t_ref, b_ref, c_ref, cs_out_ref):
    rows = [cs_ref[:, w * CONV_DIM:(w + 1) * CONV_DIM] for w in range(CONV_W - 1)] + [xbc_ref[...]]
    conv = cb_ref[...]
    for w in range(CONV_W):
        conv = conv + rows[w] * cw_ref[w:w + 1, :]
    for w in range(CONV_W - 1):
        cs_out_ref[:, w * CONV_DIM:(w + 1) * CONV_DIM] = rows[w + 1]
    xbc = _silu(conv)
    xs = xbc[:, :SSM_WIDTH]
    dt = _softplus(dt_ref[...] + dtb_ref[...])
    dec = jnp.exp(dt * (-jnp.exp(alog_ref[...])))
    dt_e = jnp.dot(dt, e64_ref[...], precision=HIGHEST, preferred_element_type=F32)
    xs_ref[...] = xs
    xdt_t_ref[...] = (xs * dt_e).T
    dec_t_ref[...] = dec.T
    b_ref[...] = xbc[:, SSM_WIDTH:SSM_WIDTH + BC_WIDTH]
    c_ref[...] = xbc[:, SSM_WIDTH + BC_WIDTH:]


def _ssd_sample_pre(xbc, conv_state, dt, lw):
    n = xbc.shape[0]
    e64, _ = _expanders()
    sds = lambda *shape: jax.ShapeDtypeStruct(shape, F32)
    return pl.pallas_call(
        _ssd_sample_pre_body,
        out_shape=[sds(n, SSM_WIDTH), sds(SSM_WIDTH, n), sds(LANES, n), sds(n, BC_WIDTH),
                   sds(n, BC_WIDTH), sds(n, (CONV_W - 1) * CONV_DIM)],
        compiler_params=pltpu.CompilerParams(vmem_limit_bytes=VMEM_LIMIT),
        name="ssd_sample_pre",
    )(xbc, conv_state.reshape(n, (CONV_W - 1) * CONV_DIM), dt, lw["conv_w"], lw["conv_b"],
      lw["dt_bias"], lw["a_log"], e64)


def _ssd_sample_state_body(xdt_t_ref, dec_t_ref, b_ref, c_ref, h_ref, hout_ref, y_ref):
    b = pl.program_id(0)
    n = xdt_t_ref.shape[1]
    pick = (lax.broadcasted_iota(jnp.int32, (n, LANES), 0) == b).astype(F32)
    xdt_col = jnp.dot(xdt_t_ref[...], pick, precision=HIGHEST, preferred_element_type=F32)
    dec_col = jnp.dot(dec_t_ref[...], pick, precision=HIGHEST, preferred_element_type=F32)
    b_row = b_ref[0]
    c_row = c_ref[0]
    grp_id = lax.broadcasted_iota(jnp.int32, (SUBLANES, D_STATE), 0)
    c_rows = jnp.zeros((SUBLANES, D_STATE), F32)
    for grp in range(SSM_GROUPS):
        c_rows = jnp.where(grp_id == grp, c_row[:, grp * D_STATE:(grp + 1) * D_STATE], c_rows)
    for h in range(SSM_HEADS):
        grp = h // HEADS_PER_GROUP
        hs = slice(h * SSM_HEAD_DIM, (h + 1) * SSM_HEAD_DIM)
        hout_ref[0, h] = (dec_col[h:h + 1, :] * h_ref[0, h]
                          + xdt_col[hs, :] * b_row[:, grp * D_STATE:(grp + 1) * D_STATE])
    h_all = hout_ref[0].reshape(SSM_WIDTH, D_STATE).astype(BF16)
    y_all = lax.dot_general(c_rows.astype(BF16), h_all, _NT, preferred_element_type=F32)
    row_grp = lax.broadcasted_iota(jnp.int32, (SUBLANES, SSM_WIDTH), 0)
    col_grp = lax.broadcasted_iota(jnp.int32, (SUBLANES, SSM_WIDTH), 1) // (SSM_WIDTH // SSM_GROUPS)
    y_ref[0] = jnp.sum(jnp.where(row_grp == col_grp, y_all, 0.0), axis=0, keepdims=True)


def _ssd_sample_state(xdt_t, dec_t, bmat, cmat, state):
    n = state.shape[0]
    const = lambda shape: pl.BlockSpec(shape, lambda b: (0,) * len(shape))
    row = lambda width: pl.BlockSpec((1, 1, width), lambda b: (b, 0, 0))
    st = pl.BlockSpec((1, SSM_HEADS, SSM_HEAD_DIM, D_STATE), lambda b: (b, 0, 0, 0))
    return pl.pallas_call(
        _ssd_sample_state_body,
        grid=(n,),
        in_specs=[const((SSM_WIDTH, n)), const((LANES, n)), row(BC_WIDTH), row(BC_WIDTH), st],
        out_specs=[st, row(SSM_WIDTH)],
        out_shape=[jax.ShapeDtypeStruct(state.shape, F32),
                   jax.ShapeDtypeStruct((n, 1, SSM_WIDTH), F32)],
        compiler_params=_params("parallel"),
        name="ssd_sample_state",
    )(xdt_t, dec_t, bmat.reshape(n, 1, BC_WIDTH), cmat.reshape(n, 1, BC_WIDTH), state)


def _ssd_sample_post_body(y_ref, xs_ref, z_ref, dskip_ref, gn_ref, o_ref):
    y = y_ref[...] + xs_ref[...] * dskip_ref[...]
    o_ref[...] = _gated_norm(y, z_ref[...], gn_ref[...]).astype(o_ref.dtype)


def _ssd_sample_post(y, xs, z, lw):
    return pl.pallas_call(
        _ssd_sample_post_body,
        out_shape=jax.ShapeDtypeStruct(y.shape, BF16),
        compiler_params=pltpu.CompilerParams(vmem_limit_bytes=VMEM_LIMIT),
        name="ssd_sample_post",
    )(y, xs, z, lw["d_skip_e"], lw["norm_ssm_out"])


def _xattn_heads(q, mk, mv):
    outs = []
    for h in range(X_HEADS):
        hs = slice(h * X_HEAD_DIM, (h + 1) * X_HEAD_DIM)
        s = lax.dot_general((q[:, hs] * (X_HEAD_DIM ** -0.5)).astype(BF16), mk[:, hs], _NT,
                            preferred_element_type=F32)
        s = s - jnp.max(s, axis=1, keepdims=True)
        p = jnp.exp(s)
        p = p / jnp.sum(p, axis=1, keepdims=True)
        outs.append(jnp.dot(p.astype(BF16), mv[:, hs], preferred_element_type=F32).astype(BF16))
    return jnp.concatenate(outs, axis=1)


def _mix_xattn_prompt_body(x_ref, att_ref, y_ref, wa_ref, wy_ref, g_ref, wq_ref, mk_ref, mv_ref,
                           wo_ref, o_ref):
    x = x_ref[...] + jnp.dot(att_ref[...], wa_ref[...], preferred_element_type=F32) \
        + jnp.dot(y_ref[...], wy_ref[...], preferred_element_type=F32)
    u = _rms(x, g_ref[...]).astype(BF16)
    q = jnp.dot(u, wq_ref[...], preferred_element_type=F32)
    o = _xattn_heads(q, mk_ref[0].astype(BF16), mv_ref[0].astype(BF16))
    o_ref[...] = x + jnp.dot(o, wo_ref[...], preferred_element_type=F32)


def _mix_xattn_prompt(x, att, y, lw, mem_k, mem_v, batch, seq):
    tm = 256
    nt = seq // tm
    n_mem = mem_k.shape[1]
    row = lambda width: pl.BlockSpec((tm, width), lambda i: (i, 0))
    const = lambda shape: pl.BlockSpec(shape, lambda i: (0,) * len(shape))
    mem = pl.BlockSpec((1, n_mem, D_MODEL), lambda i: (i // nt, 0, 0))
    return pl.pallas_call(
        _mix_xattn_prompt_body,
        grid=(batch * nt,),
        in_specs=[row(D_MODEL), row(ATT_WIDTH), row(SSM_WIDTH),
                  const((ATT_WIDTH, D_MODEL)), const((SSM_WIDTH, D_MODEL)), const((1, D_MODEL)),
                  const((D_MODEL, D_MODEL)), mem, mem, const((D_MODEL, D_MODEL))],
        out_specs=row(D_MODEL),
        out_shape=jax.ShapeDtypeStruct(x.shape, F32),
        compiler_params=_params("parallel"),
        name="mix_xattn_prompt",
    )(x, att, y, lw["w_mix_att"], lw["w_mix_ssm"], lw["norm_xattn"], lw["w_xq"], mem_k, mem_v,
      lw["w_xo"])


def _mem_kv_body(m_ref, wk_ref, wv_ref, k_ref, v_ref):
    m = m_ref[...].astype(BF16)
    k_ref[...] = jnp.dot(m, wk_ref[...], preferred_element_type=F32)
    v_ref[...] = jnp.dot(m, wv_ref[...], preferred_element_type=F32)


def _mem_kv(mem, wk, wv):
    n = mem.shape[0]
    tm = min(256, n)
    row = pl.BlockSpec((tm, D_MODEL), lambda i: (i, 0))
    const = pl.BlockSpec((D_MODEL, D_MODEL), lambda i: (0, 0))
    return pl.pallas_call(
        _mem_kv_body,
        grid=(n // tm,),
        in_specs=[row, const, const],
        out_specs=[row, row],
        out_shape=[jax.ShapeDtypeStruct((n, D_MODEL), F32)] * 2,
        compiler_params=_params("parallel"),
        name="mem_kv",
    )(mem, wk, wv)


def _mix_sample_body(x_ref, att_ref, y_ref, wa_ref, wy_ref, g_ref, wq_ref, x1_ref, q_ref):
    x = x_ref[...] + jnp.dot(att_ref[...], wa_ref[...], preferred_element_type=F32) \
        + jnp.dot(y_ref[...], wy_ref[...], preferred_element_type=F32)
    x1_ref[...] = x
    u = _rms(x, g_ref[...]).astype(BF16)
    q_ref[...] = jnp.dot(u, wq_ref[...], preferred_element_type=F32)


def _mix_sample(x, att, y, lw):
    return pl.pallas_call(
        _mix_sample_body,
        out_shape=[jax.ShapeDtypeStruct(x.shape, F32)] * 2,
        compiler_params=pltpu.CompilerParams(vmem_limit_bytes=VMEM_LIMIT),
        name="mix_sample",
    )(x, att, y, lw["w_mix_att"], lw["w_mix_ssm"], lw["norm_xattn"], lw["w_xq"])


def _xattn_sample_body(q_ref, mk_ref, mv_ref, o_ref):
    q = q_ref[0] * (X_HEAD_DIM ** -0.5)
    mv = mv_ref[0]
    prod = mk_ref[0] * q
    outs = []
    for h in range(X_HEADS):
        hs = slice(h * X_HEAD_DIM, (h + 1) * X_HEAD_DIM)
        s = jnp.sum(prod[:, hs], axis=1, keepdims=True)
        s = s - jnp.max(s, axis=0, keepdims=True)
        p = jnp.exp(s)
        p = p / jnp.sum(p, axis=0, keepdims=True)
        outs.append(jnp.sum(p * mv[:, hs], axis=0, keepdims=True))
    o_ref[0] = jnp.concatenate(outs, axis=1).astype(o_ref.dtype)


def _xattn_sample(q, mem_k, mem_v):
    n, n_mem = mem_k.shape[:2]
    row = pl.BlockSpec((1, 1, D_MODEL), lambda b: (b, 0, 0))
    mem = pl.BlockSpec((1, n_mem, D_MODEL), lambda b: (b, 0, 0))
    return pl.pallas_call(
        _xattn_sample_body,
        grid=(n,),
        in_specs=[row, mem, mem],
        out_specs=row,
        out_shape=jax.ShapeDtypeStruct((n, 1, D_MODEL), BF16),
        compiler_params=_params("parallel"),
        name="xattn_sample",
    )(q.reshape(n, 1, D_MODEL), mem_k, mem_v)


def _proj_residual_body(x_ref, a_ref, w_ref, o_ref):
    o_ref[...] = x_ref[...] + jnp.dot(a_ref[...], w_ref[...], preferred_element_type=F32)


def _proj_residual(x, a, w):
    return pl.pallas_call(
        _proj_residual_body,
        out_shape=jax.ShapeDtypeStruct(x.shape, F32),
        compiler_params=pltpu.CompilerParams(vmem_limit_bytes=VMEM_LIMIT),
        name="proj_residual",
    )(x, a, w)


def _rope_tables(pos):
    half = ROT_DIM // 2
    inv_freq = ROPE_THETA ** (-(jnp.arange(0, ROT_DIM, 2, dtype=F32) / ROT_DIM))
    ang = pos.astype(F32)[:, None] * inv_freq[None, :]
    cos, sin = jnp.cos(ang), jnp.sin(ang)
    n = pos.shape[0]
    zeros = lambda w: jnp.zeros((n, w), F32)
    cos_h = jnp.concatenate([cos, cos, jnp.ones((n, HEAD_DIM - ROT_DIM), F32)], axis=1)
    sin_hi = jnp.concatenate([-sin, zeros(HEAD_DIM - half)], axis=1)
    sin_lo = jnp.concatenate([zeros(half), sin, zeros(HEAD_DIM - ROT_DIM)], axis=1)
    two = lambda tbl: jnp.concatenate([tbl, tbl], axis=1)
    return two(cos_h), two(sin_hi), two(sin_lo)


def _layer_weights(l, p):
    row = lambda v: v[l].reshape(1, -1)
    w_mix_in = jnp.pad(p["w_mix_in"][l], ((0, 0), (0, DT_PAD - SSM_HEADS))).astype(BF16)
    w_mix_out = p["w_mix_out"][l].astype(BF16)
    return {
        "norm_ffn1": row(p["norm_ffn1"]), "ffn1_w_in": p["ffn1_w_in"][l].astype(BF16),
        "ffn1_w_out": p["ffn1_w_out"][l].astype(BF16),
        "norm_mix": row(p["norm_mix"]), "w_mix_in": w_mix_in,
        "conv_w": p["conv_w"][l], "conv_b": row(p["conv_b"]),
        "dt_bias": _pad_heads(p["dt_bias"][l]), "a_log": _pad_heads(p["a_log"][l]),
        "d_skip_e": jnp.repeat(p["d_skip"][l], SSM_HEAD_DIM).reshape(1, SSM_WIDTH),
        "norm_ssm_out": row(p["norm_ssm_out"]),
        "w_mix_att": w_mix_out[:ATT_WIDTH], "w_mix_ssm": w_mix_out[ATT_WIDTH:],
        "norm_xattn": row(p["norm_xattn"]), "w_xq": p["w_xq"][l].astype(BF16),
        "w_xk": p["w_xk"][l].astype(BF16), "w_xv": p["w_xv"][l].astype(BF16),
        "w_xo": p["w_xo"][l].astype(BF16),
        "norm_ffn2": row(p["norm_ffn2"]), "ffn2_w_in": p["ffn2_w_in"][l].astype(BF16),
        "ffn2_w_out": p["ffn2_w_out"][l].astype(BF16),
    }


def kernel(x_prompt, x_sample, mem_prompt, cache_k, cache_v, page_table, state_ssm, state_conv,
           cache_mem_k, cache_mem_v, norm_ffn1, ffn1_w_in, ffn1_w_out, norm_mix, w_mix_in, conv_w,
           conv_b, dt_bias, a_log, d_skip, norm_ssm_out, w_mix_out, norm_xattn, w_xq, w_xk, w_xv,
           w_xo, norm_ffn2, ffn2_w_in, ffn2_w_out, norm_final):
    params = dict(norm_ffn1=norm_ffn1, ffn1_w_in=ffn1_w_in, ffn1_w_out=ffn1_w_out, norm_mix=norm_mix,
                  w_mix_in=w_mix_in, conv_w=conv_w, conv_b=conv_b, dt_bias=dt_bias, a_log=a_log,
                  d_skip=d_skip, norm_ssm_out=norm_ssm_out, w_mix_out=w_mix_out,
                  norm_xattn=norm_xattn, w_xq=w_xq, w_xk=w_xk, w_xv=w_xv, w_xo=w_xo,
                  norm_ffn2=norm_ffn2, ffn2_w_in=ffn2_w_in, ffn2_w_out=ffn2_w_out)
    depth = norm_ffn1.shape[0]
    b_p, s_p, _ = x_prompt.shape
    b_s, s_s, _ = x_sample.shape
    n_mem = mem_prompt.shape[1]
    n_phys, page = cache_k.shape[1:3]
    past_len = page_table.shape[1] * page
    assert s_s == 1 and s_p % MOBA_BLOCK == 0 and past_len % MOBA_BLOCK == 0
    assert (b_p * s_p) % 512 == 0 and b_s % SUBLANES == 0

    tab_p = _rope_tables(jnp.arange(s_p, dtype=jnp.int32))
    tab_s = _rope_tables(jnp.full((b_s,), past_len, jnp.int32))
    g_final = norm_final.reshape(1, D_MODEL)
    mem_flat = mem_prompt.reshape(b_p * n_mem, D_MODEL)

    xp = x_prompt.reshape(b_p * s_p, D_MODEL)
    xs = x_sample.reshape(b_s, D_MODEL)
    outs = {k: [] for k in ("kp", "vp", "ks", "vs", "hp", "hs", "cp", "cs", "mk", "mv")}
    for l in range(depth):
        lw = _layer_weights(l, params)
        last = l == depth - 1

        mem_k, mem_v = _mem_kv(mem_flat, lw["w_xk"], lw["w_xv"])
        xp = _ffn(xp, lw["norm_ffn1"], lw["ffn1_w_in"], lw["ffn1_w_out"], g_final, False)
        q, k, v, kb, vb, z, xbc, dt, ksum = _inproj(xp, lw["norm_mix"], lw["w_mix_in"], *tab_p,
                                                    tm=MOBA_BLOCK)
        kmean = (ksum * (1.0 / MOBA_BLOCK)).reshape(b_p, s_p // MOBA_BLOCK, ATT_WIDTH)
        att = _moba_prompt(q, kb, vb, kmean, b_p, s_p)
        y, h_p = _ssd_prompt(xbc, z, dt, lw, b_p, s_p)
        xp = _mix_xattn_prompt(xp, att, y, lw, mem_k.reshape(b_p, n_mem, D_MODEL),
                               mem_v.reshape(b_p, n_mem, D_MODEL), b_p, s_p)
        xp = _ffn(xp, lw["norm_ffn2"], lw["ffn2_w_in"], lw["ffn2_w_out"], g_final, last)
        outs["kp"].append(k.reshape(b_p, s_p, ATT_HEADS, HEAD_DIM))
        outs["vp"].append(v.reshape(b_p, s_p, ATT_HEADS, HEAD_DIM))
        outs["hp"].append(h_p)
        outs["cp"].append(xbc.reshape(b_p, s_p, CONV_DIM)[:, s_p - (CONV_W - 1):])
        outs["mk"].append(mem_k.reshape(b_p, n_mem, X_HEADS, X_HEAD_DIM))
        outs["mv"].append(mem_v.reshape(b_p, n_mem, X_HEADS, X_HEAD_DIM))

        xs = _ffn(xs, lw["norm_ffn1"], lw["ffn1_w_in"], lw["ffn1_w_out"], g_final, False)
        q, k, v, _, _, z, xbc, dt, _ = _inproj(xs, lw["norm_mix"], lw["w_mix_in"], *tab_s, tm=b_s)
        att = _moba_sample(page_table, q, k, v,
                           cache_k[l].reshape(n_phys, page, ATT_WIDTH),
                           cache_v[l].reshape(n_phys, page, ATT_WIDTH)).reshape(b_s, ATT_WIDTH)
        xs_ssm, xdt_t, dec_t, bmat, cmat, conv_new = _ssd_sample_pre(xbc, state_conv[l], dt, lw)
        h_s, y = _ssd_sample_state(xdt_t, dec_t, bmat, cmat, state_ssm[l])
        y = _ssd_sample_post(y.reshape(b_s, SSM_WIDTH), xs_ssm, z, lw)
        xs, xq = _mix_sample(xs, att, y, lw)
        o = _xattn_sample(xq, cache_mem_k[l].reshape(b_s, n_mem, D_MODEL),
                          cache_mem_v[l].reshape(b_s, n_mem, D_MODEL)).reshape(b_s, D_MODEL)
        xs = _proj_residual(xs, o, lw["w_xo"])
        xs = _ffn(xs, lw["norm_ffn2"], lw["ffn2_w_in"], lw["ffn2_w_out"], g_final, last)
        outs["ks"].append(k.reshape(b_s, 1, ATT_HEADS, HEAD_DIM))
        outs["vs"].append(v.reshape(b_s, 1, ATT_HEADS, HEAD_DIM))
        outs["hs"].append(h_s)
        outs["cs"].append(conv_new.reshape(b_s, CONV_W - 1, CONV_DIM))

    st = lambda key: jnp.stack(outs[key])
    return (xp.reshape(b_p, s_p, D_MODEL), xs.reshape(b_s, 1, D_MODEL),
            st("kp"), st("vp"), st("ks"), st("vs"), st("hp"), st("hs"), st("cp"), st("cs"),
            st("mk"), st("mv"))
```

```python
import functools
import math

import jax
import jax.numpy as jnp
from jax import lax
from jax.experimental import pallas as pl
from jax.experimental.pallas import tpu as pltpu

F32 = jnp.float32
BF16 = jnp.bfloat16

D_MODEL = 1024
ATT_HEADS = 8
HEAD_DIM = 64
ATT_WIDTH = ATT_HEADS * HEAD_DIM
ROT_DIM = HEAD_DIM // 4
ROPE_THETA = 500000.0
MOBA_BLOCK = 256
MOBA_TOPK = 3
SSM_HEADS = 24
SSM_HEAD_DIM = 64
SSM_WIDTH = SSM_HEADS * SSM_HEAD_DIM
SSM_GROUPS = 4
HEADS_PER_GROUP = SSM_HEADS // SSM_GROUPS
D_STATE = 128
CONV_W = 4
BC_WIDTH = SSM_GROUPS * D_STATE
CONV_DIM = SSM_WIDTH + 2 * BC_WIDTH
SSD_CHUNK = 128
MIX_WIDTH = ATT_WIDTH + SSM_WIDTH
X_HEADS = 4
X_HEAD_DIM = D_MODEL // X_HEADS
D_FF = 2816
EPS = 1e-6

LANES = 128
SUBLANES = 8
VMEM_LIMIT = 56 * 1024 * 1024

DT_PAD = LANES
IN_PROJ_PAD = 3 * ATT_WIDTH + SSM_WIDTH + CONV_DIM + DT_PAD
NEG_BIG = -1e30
HIGHEST = lax.Precision.HIGHEST

_NT = (((1,), (1,)), ((), ()))
_TN = (((0,), (0,)), ((), ()))


def _params(*sem):
    return pltpu.CompilerParams(dimension_semantics=sem, vmem_limit_bytes=VMEM_LIMIT)


def _rms(x, g):
    return x * lax.rsqrt(jnp.mean(x * x, axis=-1, keepdims=True) + EPS) * g


def _silu(x):
    return x * jax.nn.sigmoid(x)


def _softplus(x):
    return jnp.maximum(x, 0.0) + jnp.log1p(jnp.exp(-jnp.abs(x)))


def _ffn_body(x_ref, g_ref, wg_ref, wu_ref, wo_ref, gf_ref, o_ref, u_sc, acc_sc, *, final_norm):
    j = pl.program_id(1)

    @pl.when(j == 0)
    def _():
        u_sc[...] = _rms(x_ref[...], g_ref[...]).astype(BF16)
        acc_sc[...] = jnp.zeros_like(acc_sc)

    u = u_sc[...]
    gate = jnp.dot(u, wg_ref[...], preferred_element_type=F32)
    up = jnp.dot(u, wu_ref[...], preferred_element_type=F32)
    h = (_silu(gate) * up).astype(BF16)
    acc_sc[...] += jnp.dot(h, wo_ref[...], preferred_element_type=F32)

    @pl.when(j == pl.num_programs(1) - 1)
    def _():
        y = x_ref[...] + 0.5 * acc_sc[...]
        if final_norm:
            y = _rms(y, gf_ref[...])
        o_ref[...] = y


def _ffn(x, g, w_in, w_out, g_final, final_norm):
    n = x.shape[0]
    tm = min(512, n)
    nf = 2
    tf = D_FF // nf
    return pl.pallas_call(
        functools.partial(_ffn_body, final_norm=final_norm),
        grid=(n // tm, nf),
        in_specs=[
            pl.BlockSpec((tm, D_MODEL), lambda i, j: (i, 0)),
            pl.BlockSpec((1, D_MODEL), lambda i, j: (0, 0)),
            pl.BlockSpec((D_MODEL, tf), lambda i, j: (0, j)),
            pl.BlockSpec((D_MODEL, tf), lambda i, j: (0, j + nf)),
            pl.BlockSpec((tf, D_MODEL), lambda i, j: (j, 0)),
            pl.BlockSpec((1, D_MODEL), lambda i, j: (0, 0)),
        ],
        out_specs=pl.BlockSpec((tm, D_MODEL), lambda i, j: (i, 0)),
        out_shape=jax.ShapeDtypeStruct((n, D_MODEL), F32),
        scratch_shapes=[pltpu.VMEM((tm, D_MODEL), BF16), pltpu.VMEM((tm, D_MODEL), F32)],
        compiler_params=_params("parallel", "arbitrary"),
        name="ffn",
    )(x, g, w_in, w_in, w_out, g_final)


def _rope(x, cos, sin_hi, sin_lo):
    outs = []
    for c in range(ATT_WIDTH // LANES):
        xc = x[:, c * LANES:(c + 1) * LANES]
        from_hi = pltpu.roll(xc, LANES - ROT_DIM // 2, 1)
        from_lo = pltpu.roll(xc, ROT_DIM // 2, 1)
        outs.append(xc * cos + from_hi * sin_hi + from_lo * sin_lo)
    return jnp.concatenate(outs, axis=1)


def _inproj_body(x_ref, g_ref, w_ref, cos_ref, shi_ref, slo_ref,
                 q_ref, k_ref, v_ref, kb_ref, vb_ref, z_ref, xbc_ref, dt_ref, ksum_ref):
    u = _rms(x_ref[...], g_ref[...]).astype(BF16)

    def proj(lo, hi):
        return jnp.dot(u, w_ref[:, lo:hi], preferred_element_type=F32)

    cos, shi, slo = cos_ref[...], shi_ref[...], slo_ref[...]
    a = ATT_WIDTH
    q_ref[...] = _rope(proj(0, a), cos, shi, slo)
    k = _rope(proj(a, 2 * a), cos, shi, slo)
    k_ref[...] = k
    kb_ref[...] = k.astype(BF16)
    ksum_ref[0] = jnp.sum(k, axis=0, keepdims=True)
    v = proj(2 * a, 3 * a)
    v_ref[...] = v
    vb_ref[...] = v.astype(BF16)
    o = 3 * a
    z_ref[...] = proj(o, o + SSM_WIDTH)
    o += SSM_WIDTH
    xbc_ref[...] = proj(o, o + CONV_DIM)
    o += CONV_DIM
    dt_ref[...] = proj(o, o + DT_PAD)


def _inproj(x, g, w, cos, shi, slo, tm):
    n = x.shape[0]
    nt = n // tm
    ntab = cos.shape[0] // tm
    row = lambda width: pl.BlockSpec((tm, width), lambda i: (i, 0))
    tab = pl.BlockSpec((tm, LANES), lambda i: (i % ntab, 0))
    sds = lambda width, dt: jax.ShapeDtypeStruct((n, width), dt)
    return pl.pallas_call(
        _inproj_body,
        grid=(nt,),
        in_specs=[row(D_MODEL), pl.BlockSpec((1, D_MODEL), lambda i: (0, 0)),
                  pl.BlockSpec((D_MODEL, IN_PROJ_PAD), lambda i: (0, 0)), tab, tab, tab],
        out_specs=[row(ATT_WIDTH)] * 5 + [row(SSM_WIDTH), row(CONV_DIM), row(DT_PAD),
                                           pl.BlockSpec((1, 1, ATT_WIDTH), lambda i: (i, 0, 0))],
        out_shape=[sds(ATT_WIDTH, F32), sds(ATT_WIDTH, F32), sds(ATT_WIDTH, F32),
                   sds(ATT_WIDTH, BF16), sds(ATT_WIDTH, BF16),
                   sds(SSM_WIDTH, F32), sds(CONV_DIM, F32), sds(DT_PAD, F32),
                   jax.ShapeDtypeStruct((nt, 1, ATT_WIDTH), F32)],
        compiler_params=_params("parallel"),
        name="inproj",
    )(x, g, w, cos, shi, slo)


def _moba_prompt_body(q_ref, k_ref, v_ref, kmean_ref, o_ref, qa_sc, m_sc, l_sc, acc_sc, *, nb):
    i = pl.program_id(1)
    bs = MOBA_BLOCK
    npairs = ATT_HEADS // 2
    lane = lax.broadcasted_iota(jnp.int32, (1, LANES), 1)
    lo_f = lane < HEAD_DIM
    head_mask = (jnp.where(lo_f, 1.0, 0.0), jnp.where(lo_f, 0.0, 1.0))
    head_mask_b = tuple(hm.astype(BF16) for hm in head_mask)
    bias_off = (HEAD_DIM, 0)
    qi = lax.broadcasted_iota(jnp.int32, (bs, bs), 0)
    ki = lax.broadcasted_iota(jnp.int32, (bs, bs), 1)
    causal_bias = jnp.where(ki <= qi, 0.0, NEG_BIG).astype(F32)
    eye = (qi == ki).astype(BF16)
    blk_row = lax.broadcasted_iota(jnp.int32, (nb, bs), 0)

    for pair in range(npairs):
        cols = slice(pair * LANES, (pair + 1) * LANES)
        q_pair = q_ref[:, cols]
        km_pair = kmean_ref[0, :, cols]
        for e in range(2):
            q_m = q_pair * head_mask[e]
            gt = lax.dot_general(km_pair, q_m, _NT, precision=HIGHEST,
                                 preferred_element_type=F32)
            cnt = jnp.zeros((nb, bs), F32)
            for m in range(nb):
                row = gt[m:m + 1, :]
                beats = (row > gt) | ((row == gt) & (m < blk_row))
                cnt = cnt + jnp.where(beats, 1.0, 0.0) * jnp.where(m < i, 1.0, 0.0)
            chosen = (cnt < MOBA_TOPK) & (blk_row < i)
            parts = [jnp.where(chosen, 0.0, NEG_BIG),
                     jnp.zeros((LANES - nb - bias_off[e], bs), F32)]
            if bias_off[e]:
                parts.insert(0, jnp.zeros((bias_off[e], bs), F32))
            bias_t = jnp.concatenate(parts, axis=0)
            sel = lax.dot_general(eye, bias_t.astype(BF16), _NT, preferred_element_type=F32)
            qa_sc[2 * pair + e] = (q_m * (HEAD_DIM ** -0.5) + sel).astype(BF16)

    def update(pair, s_pair, v_blk, first):
        alphas, pv = [], None
        for e in range(2):
            h = 2 * pair + e
            s_lo, s_hi = s_pair[e][:, :LANES], s_pair[e][:, LANES:]
            row_max = jnp.max(jnp.maximum(s_lo, s_hi), axis=1, keepdims=True)
            if first:
                m_new = jnp.broadcast_to(row_max, (bs, LANES))
            else:
                m_prev = m_sc[h]
                m_new = jnp.maximum(m_prev, row_max)
                alpha = jnp.exp(m_prev - m_new)
                alphas.append(alpha)
            p_lo = jnp.exp(s_lo - m_new)
            p_hi = jnp.exp(s_hi - m_new)
            l_sc[h] = (p_lo + p_hi) if first else alpha * l_sc[h] + (p_lo + p_hi)
            m_sc[h] = m_new
            p = jnp.concatenate([p_lo, p_hi], axis=1).astype(BF16)
            contrib = jnp.dot(p, v_blk * head_mask_b[e], preferred_element_type=F32)
            pv = contrib if pv is None else pv + contrib
        if first:
            acc_sc[pair] = pv
        else:
            acc_sc[pair] = acc_sc[pair] * jnp.where(lo_f, alphas[0], alphas[1]) + pv

    own_start = pl.multiple_of(i * bs, bs)
    for pair in range(npairs):
        cols = slice(pair * LANES, (pair + 1) * LANES)
        k_blk = k_ref[pl.ds(own_start, bs), cols]
        s_pair = [lax.dot_general(qa_sc[2 * pair + e], k_blk * head_mask_b[e], _NT,
                                  preferred_element_type=F32) + causal_bias for e in range(2)]
        update(pair, s_pair, v_ref[pl.ds(own_start, bs), cols], True)

    def past(n, carry):
        start = pl.multiple_of(n * bs, bs)
        for pair in range(npairs):
            cols = slice(pair * LANES, (pair + 1) * LANES)
            k_blk = k_ref[pl.ds(start, bs), cols]
            s_pair = []
            for e in range(2):
                onehot = jnp.where(lane == bias_off[e] + n, 1.0, 0.0).astype(BF16)
                k_aug = k_blk * head_mask_b[e] + onehot
                s_pair.append(lax.dot_general(qa_sc[2 * pair + e], k_aug, _NT,
                                              preferred_element_type=F32))
            update(pair, s_pair, v_ref[pl.ds(start, bs), cols], False)
        return carry

    lax.fori_loop(0, i, past, 0)

    for pair in range(npairs):
        cols = slice(pair * LANES, (pair + 1) * LANES)
        l0 = jnp.sum(l_sc[2 * pair], axis=1, keepdims=True)
        l1 = jnp.sum(l_sc[2 * pair + 1], axis=1, keepdims=True)
        o_ref[:, cols] = (acc_sc[pair] * jnp.where(lo_f, 1.0 / l0, 1.0 / l1)).astype(o_ref.dtype)


def _moba_prompt(q, kb, vb, kmean, batch, seq):
    nb = seq // MOBA_BLOCK
    return pl.pallas_call(
        functools.partial(_moba_prompt_body, nb=nb),
        grid=(batch, nb),
        in_specs=[
            pl.BlockSpec((MOBA_BLOCK, ATT_WIDTH), lambda b, i: (b * nb + i, 0)),
            pl.BlockSpec((seq, ATT_WIDTH), lambda b, i: (b, 0)),
            pl.BlockSpec((seq, ATT_WIDTH), lambda b, i: (b, 0)),
            pl.BlockSpec((1, nb, ATT_WIDTH), lambda b, i: (b, 0, 0)),
        ],
        out_specs=pl.BlockSpec((MOBA_BLOCK, ATT_WIDTH), lambda b, i: (b * nb + i, 0)),
        out_shape=jax.ShapeDtypeStruct((batch * seq, ATT_WIDTH), BF16),
        scratch_shapes=[pltpu.VMEM((ATT_HEADS, MOBA_BLOCK, LANES), BF16),
                        pltpu.VMEM((ATT_HEADS, MOBA_BLOCK, LANES), F32),
                        pltpu.VMEM((ATT_HEADS, MOBA_BLOCK, LANES), F32),
                        pltpu.VMEM((ATT_HEADS // 2, MOBA_BLOCK, LANES), F32)],
        compiler_params=_params("parallel", "arbitrary"),
        name="moba_prompt",
    )(q, kb, vb, kmean)


def _moba_sample_body(pt_ref, qkv_ref, *refs, n_pages):
    del pt_ref
    k_refs, v_refs, o_ref = refs[:n_pages], refs[n_pages:2 * n_pages], refs[2 * n_pages]
    b = pl.program_id(0)
    nseq = qkv_ref.shape[1]
    page = k_refs[0].shape[-1]
    scale = HEAD_DIM ** -0.5
    ppb = MOBA_BLOCK // page
    nb = n_pages // ppb

    pick = (lax.broadcasted_iota(jnp.int32, (nseq, page), 0) == b).astype(F32)
    cols = jnp.dot(qkv_ref[...], pick, precision=HIGHEST, preferred_element_type=F32)
    q_col, k_col, v_col = (cols[j * ATT_WIDTH:(j + 1) * ATT_WIDTH] for j in range(3))
    head_row = lax.broadcasted_iota(jnp.int32, (ATT_HEADS, page), 0)

    def head_sums(prod):
        out = jnp.zeros((ATT_HEADS, page), F32)
        for h in range(ATT_HEADS):
            s = jnp.sum(prod[h * HEAD_DIM:(h + 1) * HEAD_DIM], axis=0, keepdims=True)
            out = jnp.where(head_row == h, s, out)
        return out

    s_pages = [head_sums(q_col * k_refs[j][0, 0].reshape(ATT_WIDTH, page)) for j in range(n_pages)]

    gates, m_blk = [], []
    for n in range(nb):
        tot = mx = s_pages[n * ppb]
        for extra in s_pages[n * ppb + 1:(n + 1) * ppb]:
            tot = tot + extra
            mx = jnp.maximum(mx, extra)
        gates.append(jnp.sum(tot, axis=1, keepdims=True))
        m_blk.append(jnp.max(mx, axis=1, keepdims=True) * scale)
    s_own = head_sums(q_col * k_col)[:, 0:1] * scale
    m_tot = s_own
    chosen = []
    for n in range(nb):
        cnt = jnp.zeros((ATT_HEADS, 1), F32)
        for c in range(nb):
            if c != n:
                beats = (gates[c] > gates[n]) | ((gates[c] == gates[n]) & (c < n))
                cnt = cnt + jnp.where(beats, 1.0, 0.0)
        chosen.append(cnt < MOBA_TOPK)
        m_tot = jnp.where(chosen[n], jnp.maximum(m_tot, m_blk[n]), m_tot)
    w_own = jnp.exp(s_own - m_tot)
    l_tot = w_own
    p_pages = []
    for j in range(n_pages):
        p = jnp.where(chosen[j // ppb], jnp.exp(s_pages[j] * scale - m_tot), 0.0)
        l_tot = l_tot + jnp.sum(p, axis=1, keepdims=True)
        p_pages.append(p)

    outs = []
    for h in range(ATT_HEADS):
        acc = jnp.zeros((HEAD_DIM, page), F32)
        for j in range(n_pages):
            acc = acc + v_refs[j][0, 0, h] * p_pages[j][h:h + 1, :]
        o_h = (jnp.sum(acc, axis=1, keepdims=True)
               + w_own[h:h + 1, :] * v_col[h * HEAD_DIM:(h + 1) * HEAD_DIM, 0:1])
        outs.append(o_h / l_tot[h:h + 1, :])
    col = jnp.concatenate(outs, axis=0)

    @pl.when(b == 0)
    def _():
        o_ref[...] = jnp.zeros_like(o_ref)

    seq_lane = lax.broadcasted_iota(jnp.int32, o_ref.shape, 1)
    o_ref[...] += jnp.where(seq_lane == b, col, 0.0)


def _moba_sample(layer, page_table, qkv_t, cache_k_t, cache_v_t):
    nseq, n_pages = page_table.shape
    page = cache_k_t.shape[-1]
    assert page == LANES and MOBA_BLOCK % page == 0 and (n_pages * page) % MOBA_BLOCK == 0
    pg = lambda j: pl.BlockSpec((1, 1, ATT_HEADS, HEAD_DIM, page),
                                lambda b, pt: (layer, pt[b, j], 0, 0, 0))
    pages = [pg(j) for j in range(n_pages)]
    grid_spec = pltpu.PrefetchScalarGridSpec(
        num_scalar_prefetch=1,
        grid=(nseq,),
        in_specs=[pl.BlockSpec((3 * ATT_WIDTH, nseq), lambda b, pt: (0, 0))] + pages + pages,
        out_specs=pl.BlockSpec((ATT_WIDTH, nseq), lambda b, pt: (0, 0)),
    )
    return pl.pallas_call(
        functools.partial(_moba_sample_body, n_pages=n_pages),
        grid_spec=grid_spec,
        out_shape=jax.ShapeDtypeStruct((ATT_WIDTH, nseq), F32),
        compiler_params=_params("arbitrary"),
        name="moba_sample",
    )(page_table, qkv_t, *([cache_k_t] * n_pages), *([cache_v_t] * n_pages))


def _gated_norm(y, z, g):
    y = y * _silu(z)
    gw = SSM_WIDTH // SSM_GROUPS
    outs = []
    for grp in range(SSM_GROUPS):
        yg = y[:, grp * gw:(grp + 1) * gw]
        outs.append(yg * lax.rsqrt(jnp.mean(yg * yg, axis=-1, keepdims=True) + EPS))
    return jnp.concatenate(outs, axis=1) * g


def _ssd_prompt_body(xbc_ref, z_ref, dt_ref, cw_ref, cb_ref, dtb_ref, alog_ref, dskip_ref,
                     gn_ref, e64_ref, e128_ref, y_ref, hout_ref, xp_sc, h_sc):
    c = pl.program_id(1)
    t = SSD_CHUNK
    pad = SUBLANES

    @pl.when(c == 0)
    def _():
        xp_sc[0:pad, :] = jnp.zeros((pad, CONV_DIM), F32)
        h_sc[...] = jnp.zeros_like(h_sc)

    xp_sc[pad:pad + t, :] = xbc_ref[...]
    conv = cb_ref[...]
    for w in range(CONV_W):
        conv = conv + xp_sc[pad - (CONV_W - 1) + w:pad - (CONV_W - 1) + w + t, :] * cw_ref[w:w + 1, :]
    xp_sc[pad - (CONV_W - 1):pad, :] = xp_sc[pad + t - (CONV_W - 1):pad + t, :]
    xbc = _silu(conv)
    xs = xbc[:, :SSM_WIDTH]
    bmat = xbc[:, SSM_WIDTH:SSM_WIDTH + BC_WIDTH]
    cmat = xbc[:, SSM_WIDTH + BC_WIDTH:]

    dt = _softplus(dt_ref[...] + dtb_ref[...])
    a = dt * (-jnp.exp(alog_ref[...]))
    ti = lax.broadcasted_iota(jnp.int32, (t, t), 0)
    si = lax.broadcasted_iota(jnp.int32, (t, t), 1)
    causal = si <= ti
    tril = causal.astype(F32)
    acum = jnp.dot(tril, a, precision=HIGHEST, preferred_element_type=F32)
    acum_t = acum.T
    dt_e = jnp.dot(dt, e64_ref[...], precision=HIGHEST, preferred_element_type=F32)
    ac_e = jnp.dot(acum, e64_ref[...], precision=HIGHEST, preferred_element_type=F32)
    ac_w = jnp.dot(acum, e128_ref[...], precision=HIGHEST, preferred_element_type=F32)
    xdt = xs * dt_e
    exp_ac = jnp.exp(ac_e)
    ac_last = ac_e[t - 1:t, :]
    xw = (xdt * jnp.exp(ac_last - ac_e)).astype(BF16)
    chunk_decay = jnp.exp(ac_w[t - 1:t, :])
    xdt_b = xdt.astype(BF16)

    ys = []
    for grp in range(SSM_GROUPS):
        b_g = bmat[:, grp * D_STATE:(grp + 1) * D_STATE].astype(BF16)
        c_g = cmat[:, grp * D_STATE:(grp + 1) * D_STATE].astype(BF16)
        cb = lax.dot_general(c_g, b_g, _NT, preferred_element_type=F32)
        for r in range(HEADS_PER_GROUP):
            h = grp * HEADS_PER_GROUP + r
            hs = slice(h * SSM_HEAD_DIM, (h + 1) * SSM_HEAD_DIM)
            seg = ac_w[:, h * LANES:(h + 1) * LANES] - acum_t[h:h + 1, :]
            lmat = (cb * jnp.exp(jnp.where(causal, seg, -jnp.inf))).astype(BF16)
            y_diag = jnp.dot(lmat, xdt_b[:, hs], preferred_element_type=F32)
            h_prev = h_sc[h]
            y_off = lax.dot_general(c_g, h_prev.astype(BF16), _NT,
                                    preferred_element_type=F32) * exp_ac[:, hs]
            ys.append(y_diag + y_off)
            st = lax.dot_general(xw[:, hs], b_g, _TN, preferred_element_type=F32)
            h_sc[h] = chunk_decay[:, h * LANES:(h + 1) * LANES] * h_prev + st
    y = jnp.concatenate(ys, axis=1) + xs * dskip_ref[...]
    y_ref[...] = _gated_norm(y, z_ref[...], gn_ref[...]).astype(y_ref.dtype)

    @pl.when(c == pl.num_programs(1) - 1)
    def _():
        hout_ref[0] = h_sc[...]


def _expanders():
    heads = jnp.arange(LANES)[:, None]
    e64 = (heads == (jnp.arange(SSM_WIDTH)[None, :] // SSM_HEAD_DIM)).astype(F32)
    e128 = (heads == (jnp.arange(SSM_HEADS * LANES)[None, :] // LANES)).astype(F32)
    return e64, e128


def _pad_heads(v):
    return jnp.pad(v, (0, LANES - SSM_HEADS)).reshape(1, LANES)


def _ssd_prompt(xbc, z, dt, lw, batch, seq):
    nc = seq // SSD_CHUNK
    t = SSD_CHUNK
    e64, e128 = _expanders()
    row = lambda width: pl.BlockSpec((t, width), lambda b, c: (b * nc + c, 0))
    const = lambda shape: pl.BlockSpec(shape, lambda b, c: (0,) * len(shape))
    return pl.pallas_call(
        _ssd_prompt_body,
        grid=(batch, nc),
        in_specs=[row(CONV_DIM), row(SSM_WIDTH), row(DT_PAD),
                  const((CONV_W, CONV_DIM)), const((1, CONV_DIM)), const((1, LANES)),
                  const((1, LANES)), const((1, SSM_WIDTH)), const((1, SSM_WIDTH)),
                  const((LANES, SSM_WIDTH)), const((LANES, SSM_HEADS * LANES))],
        out_specs=[row(SSM_WIDTH),
                   pl.BlockSpec((1, SSM_HEADS, SSM_HEAD_DIM, D_STATE), lambda b, c: (b, 0, 0, 0))],
        out_shape=[jax.ShapeDtypeStruct((batch * seq, SSM_WIDTH), BF16),
                   jax.ShapeDtypeStruct((batch, SSM_HEADS, SSM_HEAD_DIM, D_STATE), F32)],
        scratch_shapes=[pltpu.VMEM((SUBLANES + t, CONV_DIM), F32),
                        pltpu.VMEM((SSM_HEADS, SSM_HEAD_DIM, D_STATE), F32)],
        compiler_params=_params("parallel", "arbitrary"),
        name="ssd_prompt",
    )(xbc, z, dt, lw["conv_w"], lw["conv_b"], lw["dt_bias"], lw["a_log"], lw["d_skip_e"],
      lw["norm_ssm_out"], e64, e128)


def _ssd_sample_pre_body(xbc_ref, cs_ref, dt_ref, cw_ref, cb_ref, dtb_ref, alog_ref, e64_ref,
                         xs_ref, xdt_t_ref, dec_t_ref, b_ref, c_ref, cs_out_ref):
    rows = [cs_ref[0, w] for w in range(CONV_W - 1)] + [xbc_ref[...]]
    conv = cb_ref[...]
    for w in range(CONV_W):
        conv = conv + rows[w] * cw_ref[w:w + 1, :]
    for w in range(CONV_W - 1):
        cs_out_ref[w] = rows[w + 1]
    xbc = _silu(conv)
    xs = xbc[:, :SSM_WIDTH]
    dt = _softplus(dt_ref[...] + dtb_ref[...])
    dec = jnp.exp(dt * (-jnp.exp(alog_ref[...])))
    dt_e = jnp.dot(dt, e64_ref[...], precision=HIGHEST, preferred_element_type=F32)
    xs_ref[...] = xs
    xdt_t_ref[...] = (xs * dt_e).T
    dec_t_ref[...] = dec.T
    b_ref[...] = xbc[:, SSM_WIDTH:SSM_WIDTH + BC_WIDTH]
    c_ref[...] = xbc[:, SSM_WIDTH + BC_WIDTH:]


def _ssd_sample_pre(layer, xbc, conv_state_t, dt, lw):
    n = xbc.shape[0]
    e64, _ = _expanders()
    sds = lambda *shape: jax.ShapeDtypeStruct(shape, F32)
    full = lambda *shape: pl.BlockSpec(shape, lambda i: (0,) * len(shape))
    return pl.pallas_call(
        _ssd_sample_pre_body,
        grid=(1,),
        in_specs=[full(n, CONV_DIM),
                  pl.BlockSpec((1, CONV_W - 1, n, CONV_DIM), lambda i: (layer, 0, 0, 0)),
                  full(n, DT_PAD), full(CONV_W, CONV_DIM), full(1, CONV_DIM), full(1, LANES),
                  full(1, LANES), full(LANES, SSM_WIDTH)],
        out_specs=[full(n, SSM_WIDTH), full(SSM_WIDTH, n), full(LANES, n), full(n, BC_WIDTH),
                   full(n, BC_WIDTH), full(CONV_W - 1, n, CONV_DIM)],
        out_shape=[sds(n, SSM_WIDTH), sds(SSM_WIDTH, n), sds(LANES, n), sds(n, BC_WIDTH),
                   sds(n, BC_WIDTH), sds(CONV_W - 1, n, CONV_DIM)],
        compiler_params=_params("arbitrary"),
        name="ssd_sample_pre",
    )(xbc, conv_state_t, dt, lw["conv_w"], lw["conv_b"], lw["dt_bias"], lw["a_log"], e64)


def _ssd_sample_state_body(xdt_t_ref, dec_t_ref, b_ref, c_ref, h_ref, hout_ref, y_ref):
    b = pl.program_id(0)
    n = xdt_t_ref.shape[1]
    pick = (lax.broadcasted_iota(jnp.int32, (n, LANES), 0) == b).astype(F32)
    xdt_col = jnp.dot(xdt_t_ref[...], pick, precision=HIGHEST, preferred_element_type=F32)
    dec_col = jnp.dot(dec_t_ref[...], pick, precision=HIGHEST, preferred_element_type=F32)
    b_row = b_ref[0]
    c_row = c_ref[0]
    grp_id = lax.broadcasted_iota(jnp.int32, (SUBLANES, D_STATE), 0)
    c_rows = jnp.zeros((SUBLANES, D_STATE), F32)
    for grp in range(SSM_GROUPS):
        c_rows = jnp.where(grp_id == grp, c_row[:, grp * D_STATE:(grp + 1) * D_STATE], c_rows)
    for h in range(SSM_HEADS):
        grp = h // HEADS_PER_GROUP
        hs = slice(h * SSM_HEAD_DIM, (h + 1) * SSM_HEAD_DIM)
        hout_ref[0, h] = (dec_col[h:h + 1, :] * h_ref[0, 0, h]
                          + xdt_col[hs, :] * b_row[:, grp * D_STATE:(grp + 1) * D_STATE])
    h_all = hout_ref[0].reshape(SSM_WIDTH, D_STATE).astype(BF16)
    y_all = lax.dot_general(c_rows.astype(BF16), h_all, _NT, preferred_element_type=F32)
    row_grp = lax.broadcasted_iota(jnp.int32, (SUBLANES, SSM_WIDTH), 0)
    col_grp = lax.broadcasted_iota(jnp.int32, (SUBLANES, SSM_WIDTH), 1) // (SSM_WIDTH // SSM_GROUPS)
    y_ref[0] = jnp.sum(jnp.where(row_grp == col_grp, y_all, 0.0), axis=0, keepdims=True)


def _ssd_sample_state(layer, xdt_t, dec_t, bmat, cmat, state):
    n = state.shape[1]
    const = lambda shape: pl.BlockSpec(shape, lambda b: (0,) * len(shape))
    row = lambda width: pl.BlockSpec((1, 1, width), lambda b: (b, 0, 0))
    st_in = pl.BlockSpec((1, 1, SSM_HEADS, SSM_HEAD_DIM, D_STATE), lambda b: (layer, b, 0, 0, 0))
    st = pl.BlockSpec((1, SSM_HEADS, SSM_HEAD_DIM, D_STATE), lambda b: (b, 0, 0, 0))
    return pl.pallas_call(
        _ssd_sample_state_body,
        grid=(n,),
        in_specs=[const((SSM_WIDTH, n)), const((LANES, n)), row(BC_WIDTH), row(BC_WIDTH), st_in],
        out_specs=[st, row(SSM_WIDTH)],
        out_shape=[jax.ShapeDtypeStruct(state.shape[1:], F32),
                   jax.ShapeDtypeStruct((n, 1, SSM_WIDTH), F32)],
        compiler_params=_params("parallel"),
        name="ssd_sample_state",
    )(xdt_t, dec_t, bmat.reshape(n, 1, BC_WIDTH), cmat.reshape(n, 1, BC_WIDTH), state)


def _ssd_sample_post_body(y_ref, xs_ref, z_ref, dskip_ref, gn_ref, o_ref):
    y = y_ref[...] + xs_ref[...] * dskip_ref[...]
    o_ref[...] = _gated_norm(y, z_ref[...], gn_ref[...]).astype(o_ref.dtype)


def _ssd_sample_post(y, xs, z, lw):
    return pl.pallas_call(
        _ssd_sample_post_body,
        out_shape=jax.ShapeDtypeStruct(y.shape, BF16),
        compiler_params=pltpu.CompilerParams(vmem_limit_bytes=VMEM_LIMIT),
        name="ssd_sample_post",
    )(y, xs, z, lw["d_skip_e"], lw["norm_ssm_out"])


def _xattn_heads(q, mk, mv):
    outs = []
    for h in range(X_HEADS):
        hs = slice(h * X_HEAD_DIM, (h + 1) * X_HEAD_DIM)
        s = lax.dot_general((q[:, hs] * (X_HEAD_DIM ** -0.5)).astype(BF16), mk[:, hs], _NT,
                            preferred_element_type=F32)
        s = s - jnp.max(s, axis=1, keepdims=True)
        p = jnp.exp(s)
        p = p / jnp.sum(p, axis=1, keepdims=True)
        outs.append(jnp.dot(p.astype(BF16), mv[:, hs], preferred_element_type=F32).astype(BF16))
    return jnp.concatenate(outs, axis=1)


def _mix_xattn_prompt_body(x_ref, att_ref, y_ref, wa_ref, wy_ref, g_ref, wq_ref, mk_ref, mv_ref,
                           wo_ref, o_ref):
    x = x_ref[...] + jnp.dot(att_ref[...], wa_ref[...], preferred_element_type=F32) \
        + jnp.dot(y_ref[...], wy_ref[...], preferred_element_type=F32)
    u = _rms(x, g_ref[...]).astype(BF16)
    q = jnp.dot(u, wq_ref[...], preferred_element_type=F32)
    o = _xattn_heads(q, mk_ref[0].astype(BF16), mv_ref[0].astype(BF16))
    o_ref[...] = x + jnp.dot(o, wo_ref[...], preferred_element_type=F32)


def _mix_xattn_prompt(x, att, y, lw, mem_k, mem_v, batch, seq):
    tm = 256
    nt = seq // tm
    n_mem = mem_k.shape[1]
    row = lambda width: pl.BlockSpec((tm, width), lambda i: (i, 0))
    const = lambda shape: pl.BlockSpec(shape, lambda i: (0,) * len(shape))
    mem = pl.BlockSpec((1, n_mem, D_MODEL), lambda i: (i // nt, 0, 0))
    return pl.pallas_call(
        _mix_xattn_prompt_body,
        grid=(batch * nt,),
        in_specs=[row(D_MODEL), row(ATT_WIDTH), row(SSM_WIDTH),
                  const((ATT_WIDTH, D_MODEL)), const((SSM_WIDTH, D_MODEL)), const((1, D_MODEL)),
                  const((D_MODEL, D_MODEL)), mem, mem, const((D_MODEL, D_MODEL))],
        out_specs=row(D_MODEL),
        out_shape=jax.ShapeDtypeStruct(x.shape, F32),
        compiler_params=_params("parallel"),
        name="mix_xattn_prompt",
    )(x, att, y, lw["w_mix_att"], lw["w_mix_ssm"], lw["norm_xattn"], lw["w_xq"], mem_k, mem_v,
      lw["w_xo"])


def _mem_kv_body(m_ref, wk_ref, wv_ref, k_ref, v_ref):
    m = m_ref[...].astype(BF16)
    k_ref[...] = jnp.dot(m, wk_ref[...], preferred_element_type=F32)
    v_ref[...] = jnp.dot(m, wv_ref[...], preferred_element_type=F32)


def _mem_kv(mem, wk, wv):
    n = mem.shape[0]
    tm = min(256, n)
    row = pl.BlockSpec((tm, D_MODEL), lambda i: (i, 0))
    const = pl.BlockSpec((D_MODEL, D_MODEL), lambda i: (0, 0))
    return pl.pallas_call(
        _mem_kv_body,
        grid=(n // tm,),
        in_specs=[row, const, const],
        out_specs=[row, row],
        out_shape=[jax.ShapeDtypeStruct((n, D_MODEL), F32)] * 2,
        compiler_params=_params("parallel"),
        name="mem_kv",
    )(mem, wk, wv)


def _mix_sample_body(x_ref, att_t_ref, y_ref, wa_ref, wy_ref, g_ref, wq_ref, x1_ref, q_ref):
    att = att_t_ref[...].T.astype(BF16)
    x = x_ref[...] + jnp.dot(att, wa_ref[...], preferred_element_type=F32) \
        + jnp.dot(y_ref[...], wy_ref[...], preferred_element_type=F32)
    x1_ref[...] = x
    u = _rms(x, g_ref[...]).astype(BF16)
    q_ref[...] = jnp.dot(u, wq_ref[...], preferred_element_type=F32)


def _mix_sample(x, att, y, lw):
    return pl.pallas_call(
        _mix_sample_body,
        out_shape=[jax.ShapeDtypeStruct(x.shape, F32)] * 2,
        compiler_params=pltpu.CompilerParams(vmem_limit_bytes=VMEM_LIMIT),
        name="mix_sample",
    )(x, att, y, lw["w_mix_att"], lw["w_mix_ssm"], lw["norm_xattn"], lw["w_xq"])


def _xattn_sample_body(q_ref, mk_ref, mv_ref, o_ref):
    q = q_ref[0] * (X_HEAD_DIM ** -0.5)
    s = jnp.sum(mk_ref[0, 0] * q[None], axis=-1, keepdims=True)
    p = jnp.exp(s - jnp.max(s, axis=0, keepdims=True))
    denom = jnp.sum(p, axis=0)
    o_ref[0] = jnp.sum(p * mv_ref[0, 0], axis=0) / denom


def _xattn_sample(layer, q, mem_k, mem_v):
    n, n_mem = mem_k.shape[1:3]
    row = pl.BlockSpec((1, X_HEADS, X_HEAD_DIM), lambda b: (b, 0, 0))
    mem = pl.BlockSpec((1, 1, n_mem, X_HEADS, X_HEAD_DIM), lambda b: (layer, b, 0, 0, 0))
    return pl.pallas_call(
        _xattn_sample_body,
        grid=(n,),
        in_specs=[row, mem, mem],
        out_specs=row,
        out_shape=jax.ShapeDtypeStruct((n, X_HEADS, X_HEAD_DIM), F32),
        compiler_params=_params("parallel"),
        name="xattn_sample",
    )(q.reshape(n, X_HEADS, X_HEAD_DIM), mem_k, mem_v).reshape(n, D_MODEL)


def _proj_residual_body(x_ref, a_ref, w_ref, o_ref):
    o_ref[...] = x_ref[...] + jnp.dot(a_ref[...].astype(BF16), w_ref[...],
                                      preferred_element_type=F32)


def _proj_residual(x, a, w):
    return pl.pallas_call(
        _proj_residual_body,
        out_shape=jax.ShapeDtypeStruct(x.shape, F32),
        compiler_params=pltpu.CompilerParams(vmem_limit_bytes=VMEM_LIMIT),
        name="proj_residual",
    )(x, a, w)


def _rope_tables(pos):
    half = ROT_DIM // 2
    inv_freq = ROPE_THETA ** (-(jnp.arange(0, ROT_DIM, 2, dtype=F32) / ROT_DIM))
    ang = pos.astype(F32)[:, None] * inv_freq[None, :]
    cos, sin = jnp.cos(ang), jnp.sin(ang)
    n = pos.shape[0]
    zeros = lambda w: jnp.zeros((n, w), F32)
    cos_h = jnp.concatenate([cos, cos, jnp.ones((n, HEAD_DIM - ROT_DIM), F32)], axis=1)
    sin_hi = jnp.concatenate([-sin, zeros(HEAD_DIM - half)], axis=1)
    sin_lo = jnp.concatenate([zeros(half), sin, zeros(HEAD_DIM - ROT_DIM)], axis=1)
    two = lambda tbl: jnp.concatenate([tbl, tbl], axis=1)
    return two(cos_h), two(sin_hi), two(sin_lo)


def _layer_weights(l, p):
    row = lambda v: v[l].reshape(1, -1)
    w_mix_in = jnp.pad(p["w_mix_in"][l], ((0, 0), (0, DT_PAD - SSM_HEADS))).astype(BF16)
    w_mix_out = p["w_mix_out"][l].astype(BF16)
    return {
        "norm_ffn1": row(p["norm_ffn1"]), "ffn1_w_in": p["ffn1_w_in"][l].astype(BF16),
        "ffn1_w_out": p["ffn1_w_out"][l].astype(BF16),
        "norm_mix": row(p["norm_mix"]), "w_mix_in": w_mix_in,
        "conv_w": p["conv_w"][l], "conv_b": row(p["conv_b"]),
        "dt_bias": _pad_heads(p["dt_bias"][l]), "a_log": _pad_heads(p["a_log"][l]),
        "d_skip_e": jnp.repeat(p["d_skip"][l], SSM_HEAD_DIM).reshape(1, SSM_WIDTH),
        "norm_ssm_out": row(p["norm_ssm_out"]),
        "w_mix_att": w_mix_out[:ATT_WIDTH], "w_mix_ssm": w_mix_out[ATT_WIDTH:],
        "norm_xattn": row(p["norm_xattn"]), "w_xq": p["w_xq"][l].astype(BF16),
        "w_xk": p["w_xk"][l].astype(BF16), "w_xv": p["w_xv"][l].astype(BF16),
        "w_xo": p["w_xo"][l].astype(BF16),
        "norm_ffn2": row(p["norm_ffn2"]), "ffn2_w_in": p["ffn2_w_in"][l].astype(BF16),
        "ffn2_w_out": p["ffn2_w_out"][l].astype(BF16),
    }


def kernel(x_prompt, x_sample, mem_prompt, cache_k, cache_v, page_table, state_ssm, state_conv,
           cache_mem_k, cache_mem_v, norm_ffn1, ffn1_w_in, ffn1_w_out, norm_mix, w_mix_in, conv_w,
           conv_b, dt_bias, a_log, d_skip, norm_ssm_out, w_mix_out, norm_xattn, w_xq, w_xk, w_xv,
           w_xo, norm_ffn2, ffn2_w_in, ffn2_w_out, norm_final):
    params = dict(norm_ffn1=norm_ffn1, ffn1_w_in=ffn1_w_in, ffn1_w_out=ffn1_w_out, norm_mix=norm_mix,
                  w_mix_in=w_mix_in, conv_w=conv_w, conv_b=conv_b, dt_bias=dt_bias, a_log=a_log,
                  d_skip=d_skip, norm_ssm_out=norm_ssm_out, w_mix_out=w_mix_out,
                  norm_xattn=norm_xattn, w_xq=w_xq, w_xk=w_xk, w_xv=w_xv, w_xo=w_xo,
                  norm_ffn2=norm_ffn2, ffn2_w_in=ffn2_w_in, ffn2_w_out=ffn2_w_out)
    depth = norm_ffn1.shape[0]
    b_p, s_p, _ = x_prompt.shape
    b_s, s_s, _ = x_sample.shape
    n_mem = mem_prompt.shape[1]
    n_phys, page = cache_k.shape[1:3]
    past_len = page_table.shape[1] * page
    assert s_s == 1 and s_p % MOBA_BLOCK == 0 and past_len % MOBA_BLOCK == 0
    assert (b_p * s_p) % 512 == 0 and b_s % SUBLANES == 0

    tab_p = _rope_tables(jnp.arange(s_p, dtype=jnp.int32))
    tab_s = _rope_tables(jnp.full((b_s,), past_len, jnp.int32))
    g_final = norm_final.reshape(1, D_MODEL)
    mem_flat = mem_prompt.reshape(b_p * n_mem, D_MODEL)
    cache_k_t = jnp.transpose(cache_k, (0, 1, 3, 4, 2))
    cache_v_t = jnp.transpose(cache_v, (0, 1, 3, 4, 2))
    conv_state_t = jnp.transpose(state_conv, (0, 2, 1, 3))

    xp = x_prompt.reshape(b_p * s_p, D_MODEL)
    xs = x_sample.reshape(b_s, D_MODEL)
    outs = {k: [] for k in ("kp", "vp", "ks", "vs", "hp", "hs", "cp", "cs", "mk", "mv")}
    for l in range(depth):
        lw = _layer_weights(l, params)
        last = l == depth - 1

        mem_k, mem_v = _mem_kv(mem_flat, lw["w_xk"], lw["w_xv"])
        xp = _ffn(xp, lw["norm_ffn1"], lw["ffn1_w_in"], lw["ffn1_w_out"], g_final, False)
        q, k, v, kb, vb, z, xbc, dt, ksum = _inproj(xp, lw["norm_mix"], lw["w_mix_in"], *tab_p,
                                                    tm=MOBA_BLOCK)
        kmean = (ksum * (1.0 / MOBA_BLOCK)).reshape(b_p, s_p // MOBA_BLOCK, ATT_WIDTH)
        att = _moba_prompt(q, kb, vb, kmean, b_p, s_p)
        y, h_p = _ssd_prompt(xbc, z, dt, lw, b_p, s_p)
        xp = _mix_xattn_prompt(xp, att, y, lw, mem_k.reshape(b_p, n_mem, D_MODEL),
                               mem_v.reshape(b_p, n_mem, D_MODEL), b_p, s_p)
        xp = _ffn(xp, lw["norm_ffn2"], lw["ffn2_w_in"], lw["ffn2_w_out"], g_final, last)
        outs["kp"].append(k.reshape(b_p, s_p, ATT_HEADS, HEAD_DIM))
        outs["vp"].append(v.reshape(b_p, s_p, ATT_HEADS, HEAD_DIM))
        outs["hp"].append(h_p)
        outs["cp"].append(xbc.reshape(b_p, s_p, CONV_DIM)[:, s_p - (CONV_W - 1):])
        outs["mk"].append(mem_k.reshape(b_p, n_mem, X_HEADS, X_HEAD_DIM))
        outs["mv"].append(mem_v.reshape(b_p, n_mem, X_HEADS, X_HEAD_DIM))

        xs = _ffn(xs, lw["norm_ffn1"], lw["ffn1_w_in"], lw["ffn1_w_out"], g_final, False)
        q, k, v, _, _, z, xbc, dt, _ = _inproj(xs, lw["norm_mix"], lw["w_mix_in"], *tab_s, tm=b_s)
        att_t = _moba_sample(l, page_table, jnp.concatenate([q.T, k.T, v.T], axis=0),
                             cache_k_t, cache_v_t)
        xs_ssm, xdt_t, dec_t, bmat, cmat, conv_new = _ssd_sample_pre(l, xbc, conv_state_t, dt, lw)
        h_s, y = _ssd_sample_state(l, xdt_t, dec_t, bmat, cmat, state_ssm)
        y = _ssd_sample_post(y.reshape(b_s, SSM_WIDTH), xs_ssm, z, lw)
        xs, xq = _mix_sample(xs, att_t, y, lw)
        o = _xattn_sample(l, xq, cache_mem_k, cache_mem_v)
        xs = _proj_residual(xs, o, lw["w_xo"])
        xs = _ffn(xs, lw["norm_ffn2"], lw["ffn2_w_in"], lw["ffn2_w_out"], g_final, last)
        outs["ks"].append(k.reshape(b_s, 1, ATT_HEADS, HEAD_DIM))
        outs["vs"].append(v.reshape(b_s, 1, ATT_HEADS, HEAD_DIM))
        outs["hs"].append(h_s)
        outs["cs"].append(conv_new)

    st = lambda key: jnp.stack(outs[key])
    return (xp.reshape(b_p, s_p, D_MODEL), xs.reshape(b_s, 1, D_MODEL),
            st("kp"), st("vp"), st("ks"), st("vs"), st("hp"), st("hs"), st("cp"),
            jnp.transpose(st("cs"), (0, 2, 1, 3)), st("mk"), st("mv"))
```

```python
import functools
import math

import jax
import jax.numpy as jnp
from jax import lax
from jax.experimental import pallas as pl
from jax.experimental.pallas import tpu as pltpu

F32 = jnp.float32
BF16 = jnp.bfloat16

D_MODEL = 1024
ATT_HEADS = 8
HEAD_DIM = 64
ATT_WIDTH = ATT_HEADS * HEAD_DIM
ROT_DIM = HEAD_DIM // 4
ROPE_THETA = 500000.0
MOBA_BLOCK = 256
MOBA_TOPK = 3
SSM_HEADS = 24
SSM_HEAD_DIM = 64
SSM_WIDTH = SSM_HEADS * SSM_HEAD_DIM
SSM_GROUPS = 4
HEADS_PER_GROUP = SSM_HEADS // SSM_GROUPS
D_STATE = 128
CONV_W = 4
BC_WIDTH = SSM_GROUPS * D_STATE
CONV_DIM = SSM_WIDTH + 2 * BC_WIDTH
SSD_CHUNK = 128
MIX_WIDTH = ATT_WIDTH + SSM_WIDTH
X_HEADS = 4
X_HEAD_DIM = D_MODEL // X_HEADS
D_FF = 2816
EPS = 1e-6

LANES = 128
SUBLANES = 8
VMEM_LIMIT = 56 * 1024 * 1024

DT_PAD = LANES
HEAD_ROWS = -(-SSM_HEADS // SUBLANES) * SUBLANES
IN_PROJ_PAD = 3 * ATT_WIDTH + SSM_WIDTH + CONV_DIM + DT_PAD
NEG_BIG = -1e30
QK_SCALE_LOG2 = HEAD_DIM ** -0.5 * math.log2(math.e)
HIGHEST = lax.Precision.HIGHEST

_NT = (((1,), (1,)), ((), ()))
_TN = (((0,), (0,)), ((), ()))


def _params(*sem):
    return pltpu.CompilerParams(dimension_semantics=sem, vmem_limit_bytes=VMEM_LIMIT)


def _rms(x, g):
    return x * lax.rsqrt(jnp.mean(x * x, axis=-1, keepdims=True) + EPS) * g


def _silu(x):
    return x * jax.nn.sigmoid(x)


def _softplus(x):
    return jnp.maximum(x, 0.0) + jnp.log1p(jnp.exp(-jnp.abs(x)))


def _ffn_body(x_ref, g_ref, wg_ref, wu_ref, wo_ref, gf_ref, o_ref, u_sc, acc_sc, *, final_norm):
    j = pl.program_id(1)

    @pl.when(j == 0)
    def _():
        u_sc[...] = _rms(x_ref[...], g_ref[...]).astype(BF16)
        acc_sc[...] = jnp.zeros_like(acc_sc)

    u = u_sc[...]
    gate = jnp.dot(u, wg_ref[...], preferred_element_type=F32)
    up = jnp.dot(u, wu_ref[...], preferred_element_type=F32)
    h = (_silu(gate) * up).astype(BF16)
    acc_sc[...] += jnp.dot(h, wo_ref[...], preferred_element_type=F32)

    @pl.when(j == pl.num_programs(1) - 1)
    def _():
        y = x_ref[...] + 0.5 * acc_sc[...]
        if final_norm:
            y = _rms(y, gf_ref[...])
        o_ref[...] = y


def _ffn(x, g, w_in, w_out, g_final, final_norm):
    n = x.shape[0]
    tm = min(512, n)
    nf = 2
    tf = D_FF // nf
    return pl.pallas_call(
        functools.partial(_ffn_body, final_norm=final_norm),
        grid=(n // tm, nf),
        in_specs=[
            pl.BlockSpec((tm, D_MODEL), lambda i, j: (i, 0)),
            pl.BlockSpec((1, D_MODEL), lambda i, j: (0, 0)),
            pl.BlockSpec((D_MODEL, tf), lambda i, j: (0, j)),
            pl.BlockSpec((D_MODEL, tf), lambda i, j: (0, j + nf)),
            pl.BlockSpec((tf, D_MODEL), lambda i, j: (j, 0)),
            pl.BlockSpec((1, D_MODEL), lambda i, j: (0, 0)),
        ],
        out_specs=pl.BlockSpec((tm, D_MODEL), lambda i, j: (i, 0)),
        out_shape=jax.ShapeDtypeStruct((n, D_MODEL), F32),
        scratch_shapes=[pltpu.VMEM((tm, D_MODEL), BF16), pltpu.VMEM((tm, D_MODEL), F32)],
        compiler_params=_params("parallel", "arbitrary"),
        name="ffn",
    )(x, g, w_in, w_in, w_out, g_final)


def _rope(x, cos, sin_hi, sin_lo):
    outs = []
    for c in range(ATT_WIDTH // LANES):
        xc = x[:, c * LANES:(c + 1) * LANES]
        from_hi = pltpu.roll(xc, LANES - ROT_DIM // 2, 1)
        from_lo = pltpu.roll(xc, ROT_DIM // 2, 1)
        outs.append(xc * cos + from_hi * sin_hi + from_lo * sin_lo)
    return jnp.concatenate(outs, axis=1)


def _inproj_body(x_ref, g_ref, w_ref, cos_ref, shi_ref, slo_ref,
                 q_ref, kt_ref, vt_ref, kb_ref, vb_ref, z_ref, xbc_ref, dt_ref, ksum_ref):
    u = _rms(x_ref[...], g_ref[...]).astype(BF16)

    def proj(lo, hi):
        return jnp.dot(u, w_ref[:, lo:hi], preferred_element_type=F32)

    cos, shi, slo = cos_ref[...], shi_ref[...], slo_ref[...]
    a = ATT_WIDTH
    q_ref[...] = _rope(proj(0, a), cos, shi, slo)
    k = _rope(proj(a, 2 * a), cos, shi, slo)
    kt_ref[0] = k.T
    kb_ref[...] = k.astype(BF16)
    ksum_ref[0] = jnp.sum(k, axis=0, keepdims=True)
    v = proj(2 * a, 3 * a)
    vt_ref[0] = v.T
    vb_ref[...] = v.astype(BF16)
    o = 3 * a
    z_ref[...] = proj(o, o + SSM_WIDTH)
    o += SSM_WIDTH
    xbc_ref[...] = proj(o, o + CONV_DIM)
    o += CONV_DIM
    dt_ref[...] = proj(o, o + DT_PAD)


def _inproj(x, g, w, cos, shi, slo, tm, seq_len):
    n = x.shape[0]
    nt = n // tm
    tps = seq_len // tm
    ntab = cos.shape[0] // tm
    row = lambda width: pl.BlockSpec((tm, width), lambda i: (i, 0))
    tab = pl.BlockSpec((tm, LANES), lambda i: (i % ntab, 0))
    col = pl.BlockSpec((1, ATT_WIDTH, tm), lambda i: (i // tps, 0, i % tps))
    sds = lambda width, dt: jax.ShapeDtypeStruct((n, width), dt)
    kv_t = jax.ShapeDtypeStruct((n // seq_len, ATT_WIDTH, seq_len), F32)
    return pl.pallas_call(
        _inproj_body,
        grid=(nt,),
        in_specs=[row(D_MODEL), pl.BlockSpec((1, D_MODEL), lambda i: (0, 0)),
                  pl.BlockSpec((D_MODEL, IN_PROJ_PAD), lambda i: (0, 0)), tab, tab, tab],
        out_specs=[row(ATT_WIDTH), col, col, row(ATT_WIDTH), row(ATT_WIDTH),
                   row(SSM_WIDTH), row(CONV_DIM), row(DT_PAD),
                   pl.BlockSpec((1, 1, ATT_WIDTH), lambda i: (i, 0, 0))],
        out_shape=[sds(ATT_WIDTH, F32), kv_t, kv_t,
                   sds(ATT_WIDTH, BF16), sds(ATT_WIDTH, BF16),
                   sds(SSM_WIDTH, F32), sds(CONV_DIM, F32), sds(DT_PAD, F32),
                   jax.ShapeDtypeStruct((nt, 1, ATT_WIDTH), F32)],
        compiler_params=_params("parallel"),
        name="inproj",
    )(x, g, w, cos, shi, slo)


def _moba_prompt_body(q_ref, k_ref, v_ref, kmean_ref, o_ref, qa_sc, m_sc, l_sc, acc_sc, *, nb):
    i = pl.program_id(1)
    bs = MOBA_BLOCK
    npairs = ATT_HEADS // 2
    lane = lax.broadcasted_iota(jnp.int32, (1, LANES), 1)
    lo_f = lane < HEAD_DIM
    head_mask = (jnp.where(lo_f, 1.0, 0.0), jnp.where(lo_f, 0.0, 1.0))
    head_mask_b = tuple(hm.astype(BF16) for hm in head_mask)
    bias_off = (HEAD_DIM, 0)
    qi = lax.broadcasted_iota(jnp.int32, (bs, bs), 0)
    ki = lax.broadcasted_iota(jnp.int32, (bs, bs), 1)
    causal_bias = jnp.where(ki <= qi, 0.0, NEG_BIG).astype(F32)
    eye = (qi == ki).astype(BF16)
    blk_row = lax.broadcasted_iota(jnp.int32, (nb, bs), 0)

    is_past = blk_row < i
    tie_after = [jnp.where(m < blk_row, 1.0, 0.0) for m in range(nb)]
    for pair in range(npairs):
        cols = slice(pair * LANES, (pair + 1) * LANES)
        q_pair = q_ref[:, cols]
        km_pair = kmean_ref[0, :, cols]
        km_both = jnp.concatenate([km_pair * head_mask[0], km_pair * head_mask[1]], axis=0)
        gt_both = lax.dot_general(km_both, q_pair, _NT, precision=HIGHEST,
                                  preferred_element_type=F32)
        bias_rows = []
        for e in range(2):
            gt = jnp.where(is_past, gt_both[e * nb:(e + 1) * nb], -jnp.inf)
            cnt = jnp.zeros((nb, bs), F32)
            for m in range(nb):
                row = gt[m:m + 1, :]
                cnt = cnt + jnp.where(row > gt, 1.0, jnp.where(row == gt, tie_after[m], 0.0))
            bias_rows.append(jnp.where(cnt < MOBA_TOPK, jnp.where(is_past, 0.0, NEG_BIG), NEG_BIG))
        bias_t = jnp.concatenate(
            [bias_rows[1], jnp.zeros((bias_off[0] - nb, bs), F32), bias_rows[0],
             jnp.zeros((LANES - bias_off[0] - nb, bs), F32)], axis=0)
        sel = lax.dot_general(eye, bias_t.astype(BF16), _NT, preferred_element_type=F32)
        q_s = q_pair * QK_SCALE_LOG2
        for e in range(2):
            qa_sc[2 * pair + e] = (q_s * head_mask[e] + sel * head_mask[1 - e]).astype(BF16)

    def update(pair, s_pair, v_blk, first):
        alphas, pv = [], None
        for e in range(2):
            h = 2 * pair + e
            s_lo, s_hi = s_pair[e][:, :LANES], s_pair[e][:, LANES:]
            row_max = jnp.max(jnp.maximum(s_lo, s_hi), axis=1, keepdims=True)
            if first:
                m_new = jnp.broadcast_to(row_max, (bs, LANES))
            else:
                m_prev = m_sc[h]
                m_new = jnp.maximum(m_prev, row_max)
                alpha = jnp.exp2(m_prev - m_new)
                alphas.append(alpha)
            p_lo = jnp.exp2(s_lo - m_new)
            p_hi = jnp.exp2(s_hi - m_new)
            l_sc[h] = (p_lo + p_hi) if first else alpha * l_sc[h] + (p_lo + p_hi)
            m_sc[h] = m_new
            p = jnp.concatenate([p_lo, p_hi], axis=1).astype(BF16)
            contrib = jnp.dot(p, v_blk * head_mask_b[e], preferred_element_type=F32)
            pv = contrib if pv is None else pv + contrib
        if first:
            acc_sc[pair] = pv
        else:
            acc_sc[pair] = acc_sc[pair] * jnp.where(lo_f, alphas[0], alphas[1]) + pv

    own_start = pl.multiple_of(i * bs, bs)
    for pair in range(npairs):
        cols = slice(pair * LANES, (pair + 1) * LANES)
        k_blk = k_ref[pl.ds(own_start, bs), cols]
        s_pair = [lax.dot_general(qa_sc[2 * pair + e], k_blk * head_mask_b[e], _NT,
                                  preferred_element_type=F32) + causal_bias for e in range(2)]
        update(pair, s_pair, v_ref[pl.ds(own_start, bs), cols], True)

    def past(n, carry):
        start = pl.multiple_of(n * bs, bs)
        for pair in range(npairs):
            cols = slice(pair * LANES, (pair + 1) * LANES)
            k_blk = k_ref[pl.ds(start, bs), cols]
            s_pair = []
            for e in range(2):
                onehot = jnp.where(lane == bias_off[e] + n, 1.0, 0.0).astype(BF16)
                k_aug = k_blk * head_mask_b[e] + onehot
                s_pair.append(lax.dot_general(qa_sc[2 * pair + e], k_aug, _NT,
                                              preferred_element_type=F32))
            update(pair, s_pair, v_ref[pl.ds(start, bs), cols], False)
        return carry

    lax.fori_loop(0, i, past, 0)

    for pair in range(npairs):
        cols = slice(pair * LANES, (pair + 1) * LANES)
        l0 = jnp.sum(l_sc[2 * pair], axis=1, keepdims=True)
        l1 = jnp.sum(l_sc[2 * pair + 1], axis=1, keepdims=True)
        o_ref[:, cols] = (acc_sc[pair] * jnp.where(lo_f, 1.0 / l0, 1.0 / l1)).astype(o_ref.dtype)


def _moba_prompt(q, kb, vb, kmean, batch, seq):
    nb = seq // MOBA_BLOCK
    return pl.pallas_call(
        functools.partial(_moba_prompt_body, nb=nb),
        grid=(batch, nb),
        in_specs=[
            pl.BlockSpec((MOBA_BLOCK, ATT_WIDTH), lambda b, i: (b * nb + i, 0)),
            pl.BlockSpec((seq, ATT_WIDTH), lambda b, i: (b, 0)),
            pl.BlockSpec((seq, ATT_WIDTH), lambda b, i: (b, 0)),
            pl.BlockSpec((1, nb, ATT_WIDTH), lambda b, i: (b, 0, 0)),
        ],
        out_specs=pl.BlockSpec((MOBA_BLOCK, ATT_WIDTH), lambda b, i: (b * nb + i, 0)),
        out_shape=jax.ShapeDtypeStruct((batch * seq, ATT_WIDTH), BF16),
        scratch_shapes=[pltpu.VMEM((ATT_HEADS, MOBA_BLOCK, LANES), BF16),
                        pltpu.VMEM((ATT_HEADS, MOBA_BLOCK, LANES), F32),
                        pltpu.VMEM((ATT_HEADS, MOBA_BLOCK, LANES), F32),
                        pltpu.VMEM((ATT_HEADS // 2, MOBA_BLOCK, LANES), F32)],
        compiler_params=_params("parallel", "arbitrary"),
        name="moba_prompt",
    )(q, kb, vb, kmean)


def _moba_sample_body(pt_ref, qkv_ref, *refs, n_pages):
    del pt_ref
    k_refs, v_refs, o_ref = refs[:n_pages], refs[n_pages:2 * n_pages], refs[2 * n_pages]
    b = pl.program_id(0)
    nseq = qkv_ref.shape[1]
    page = k_refs[0].shape[-1]
    scale = HEAD_DIM ** -0.5
    ppb = MOBA_BLOCK // page
    nb = n_pages // ppb

    pick = (lax.broadcasted_iota(jnp.int32, (nseq, page), 0) == b).astype(BF16)
    cols = _dot3_right(qkv_ref[...], pick)
    q_col, k_col, v_col = (cols[j * ATT_WIDTH:(j + 1) * ATT_WIDTH] for j in range(3))
    head_row = lax.broadcasted_iota(jnp.int32, (ATT_HEADS, page), 0)

    def head_sums(prod):
        out = jnp.zeros((ATT_HEADS, page), F32)
        for h in range(ATT_HEADS):
            s = jnp.sum(prod[h * HEAD_DIM:(h + 1) * HEAD_DIM], axis=0, keepdims=True)
            out = jnp.where(head_row == h, s, out)
        return out

    s_pages = [head_sums(q_col * k_refs[j][0, 0].reshape(ATT_WIDTH, page)) for j in range(n_pages)]

    gates, m_blk = [], []
    for n in range(nb):
        tot = mx = s_pages[n * ppb]
        for extra in s_pages[n * ppb + 1:(n + 1) * ppb]:
            tot = tot + extra
            mx = jnp.maximum(mx, extra)
        gates.append(jnp.sum(tot, axis=1, keepdims=True))
        m_blk.append(jnp.max(mx, axis=1, keepdims=True) * scale)
    s_own = head_sums(q_col * k_col)[:, 0:1] * scale
    m_tot = s_own
    chosen = []
    for n in range(nb):
        cnt = jnp.zeros((ATT_HEADS, 1), F32)
        for c in range(nb):
            if c != n:
                beats = (gates[c] > gates[n]) | ((gates[c] == gates[n]) & (c < n))
                cnt = cnt + jnp.where(beats, 1.0, 0.0)
        chosen.append(cnt < MOBA_TOPK)
        m_tot = jnp.where(chosen[n], jnp.maximum(m_tot, m_blk[n]), m_tot)
    w_own = jnp.exp(s_own - m_tot)
    l_tot = w_own
    p_pages = []
    for j in range(n_pages):
        p = jnp.where(chosen[j // ppb], jnp.exp(s_pages[j] * scale - m_tot), 0.0)
        l_tot = l_tot + jnp.sum(p, axis=1, keepdims=True)
        p_pages.append(p)

    outs = []
    for h in range(ATT_HEADS):
        acc = jnp.zeros((HEAD_DIM, page), F32)
        for j in range(n_pages):
            acc = acc + v_refs[j][0, 0, h] * p_pages[j][h:h + 1, :]
        o_h = (jnp.sum(acc, axis=1, keepdims=True)
               + w_own[h:h + 1, :] * v_col[h * HEAD_DIM:(h + 1) * HEAD_DIM, 0:1])
        outs.append(o_h / l_tot[h:h + 1, :])
    col = jnp.concatenate(outs, axis=0)

    @pl.when(b == 0)
    def _():
        o_ref[...] = jnp.zeros_like(o_ref)

    seq_lane = lax.broadcasted_iota(jnp.int32, o_ref.shape, 1)
    o_ref[...] += jnp.where(seq_lane == b, col, 0.0)


def _moba_sample(layer, page_table, qkv_t, cache_k_t, cache_v_t):
    nseq, n_pages = page_table.shape
    page = cache_k_t.shape[-1]
    assert page == LANES and MOBA_BLOCK % page == 0 and (n_pages * page) % MOBA_BLOCK == 0
    pg = lambda j: pl.BlockSpec((1, 1, ATT_HEADS, HEAD_DIM, page),
                                lambda b, pt: (layer, pt[b, j], 0, 0, 0))
    pages = [pg(j) for j in range(n_pages)]
    grid_spec = pltpu.PrefetchScalarGridSpec(
        num_scalar_prefetch=1,
        grid=(nseq,),
        in_specs=[pl.BlockSpec((3 * ATT_WIDTH, nseq), lambda b, pt: (0, 0))] + pages + pages,
        out_specs=pl.BlockSpec((ATT_WIDTH, nseq), lambda b, pt: (0, 0)),
    )
    return pl.pallas_call(
        functools.partial(_moba_sample_body, n_pages=n_pages),
        grid_spec=grid_spec,
        out_shape=jax.ShapeDtypeStruct((ATT_WIDTH, nseq), F32),
        compiler_params=_params("arbitrary"),
        name="moba_sample",
    )(page_table, qkv_t, *([cache_k_t] * n_pages), *([cache_v_t] * n_pages))


def _gated_norm(y, z, g):
    y = y * _silu(z)
    gw = SSM_WIDTH // SSM_GROUPS
    outs = []
    for grp in range(SSM_GROUPS):
        yg = y[:, grp * gw:(grp + 1) * gw]
        outs.append(yg * lax.rsqrt(jnp.mean(yg * yg, axis=-1, keepdims=True) + EPS))
    return jnp.concatenate(outs, axis=1) * g


def _ssd_prompt_body(xbc_ref, z_ref, dt_ref, cw_ref, cb_ref, dtb_ref, alog_ref, dskip_ref,
                     gn_ref, e64_ref, e128_ref, y_ref, hout_ref, xp_sc, h_sc):
    c = pl.program_id(1)
    t = SSD_CHUNK
    pad = SUBLANES
    taps = CONV_W - 1

    @pl.when(c == 0)
    def _():
        xp_sc[0:pad, :] = jnp.zeros((pad, CONV_DIM), F32)
        h_sc[...] = jnp.zeros_like(h_sc)

    raw = xbc_ref[...]
    xp_sc[pad:pad + t, :] = raw
    conv = cb_ref[...] + raw * cw_ref[taps:taps + 1, :]
    for w in range(taps):
        conv = conv + xp_sc[pad - taps + w:pad - taps + w + t, :] * cw_ref[w:w + 1, :]
    xp_sc[pad - taps:pad, :] = raw[t - taps:, :]
    xbc = _silu(conv)
    xs = xbc[:, :SSM_WIDTH]
    bmat = xbc[:, SSM_WIDTH:SSM_WIDTH + BC_WIDTH]
    cmat = xbc[:, SSM_WIDTH + BC_WIDTH:]

    dt_t = _softplus((dt_ref[...] + dtb_ref[...]).T[:HEAD_ROWS])
    a_t = dt_t * (-jnp.exp(alog_ref[:HEAD_ROWS, :]))
    ti = lax.broadcasted_iota(jnp.int32, (t, t), 0)
    si = lax.broadcasted_iota(jnp.int32, (t, t), 1)
    causal = si <= ti
    acum_t = _dot3_right(a_t, (ti <= si).astype(BF16))
    fill = jnp.zeros((LANES - HEAD_ROWS, t), F32)
    dt = jnp.concatenate([dt_t, fill], axis=0).T
    acum = jnp.concatenate([acum_t, fill], axis=0).T
    dt_e = _dot3_right(dt, e64_ref[...])
    ac_w = _dot3_right(acum, e128_ref[...])
    lo_f = lax.broadcasted_iota(jnp.int32, (1, LANES), 1) < SSM_HEAD_DIM
    half_b = (jnp.where(lo_f, 1.0, 0.0).astype(BF16), jnp.where(lo_f, 0.0, 1.0).astype(BF16))

    ys = []
    for grp in range(SSM_GROUPS):
        b_f = bmat[:, grp * D_STATE:(grp + 1) * D_STATE]
        b_g = b_f.astype(BF16)
        b_gt = b_f.T.astype(BF16)
        c_g = cmat[:, grp * D_STATE:(grp + 1) * D_STATE].astype(BF16)
        cb = lax.dot_general(c_g, b_g, _NT, preferred_element_type=F32)
        for r in range(HEADS_PER_GROUP // 2):
            j = grp * (HEADS_PER_GROUP // 2) + r
            pc = slice(j * LANES, (j + 1) * LANES)
            a_h = [ac_w[:, (2 * j + e) * LANES:(2 * j + e + 1) * LANES] for e in range(2)]
            ac_pair = jnp.where(lo_f, a_h[0], a_h[1])
            ac_last = ac_pair[t - 1:t, :]
            xdt = xs[:, pc] * dt_e[:, pc]
            xw = (xdt * jnp.exp(ac_last - ac_pair)).astype(BF16)
            xdt_b = xdt.astype(BF16)
            y_diag = None
            for e in range(2):
                seg = a_h[e] - acum_t[2 * j + e:2 * j + e + 1, :]
                lmat = (cb * jnp.exp(jnp.where(causal, seg, -jnp.inf))).astype(BF16)
                part = jnp.dot(lmat, xdt_b * half_b[e], preferred_element_type=F32)
                y_diag = part if y_diag is None else y_diag + part
            h_prev = h_sc[j]
            y_off = jnp.dot(c_g, h_prev.astype(BF16), preferred_element_type=F32) * jnp.exp(ac_pair)
            ys.append(y_diag + y_off)
            h_sc[j] = jnp.exp(ac_last) * h_prev + jnp.dot(b_gt, xw, preferred_element_type=F32)
    y = jnp.concatenate(ys, axis=1) + xs * dskip_ref[...]
    y_ref[...] = _gated_norm(y, z_ref[...], gn_ref[...]).astype(y_ref.dtype)

    @pl.when(c == pl.num_programs(1) - 1)
    def _():
        for j in range(SSM_HEADS // 2):
            h_pn = h_sc[j].T
            hout_ref[0, 2 * j] = h_pn[:SSM_HEAD_DIM]
            hout_ref[0, 2 * j + 1] = h_pn[SSM_HEAD_DIM:]


def _split3(x):
    hi = x.astype(BF16)
    rest = x - hi.astype(F32)
    mid = rest.astype(BF16)
    lo = (rest - mid.astype(F32)).astype(BF16)
    return hi, mid, lo


def _dot3_right(x, sel):
    return functools.reduce(
        lambda u, v: u + v, [jnp.dot(p, sel, preferred_element_type=F32) for p in _split3(x)])


def _dot3_left(sel, x):
    return functools.reduce(
        lambda u, v: u + v, [jnp.dot(sel, p, preferred_element_type=F32) for p in _split3(x)])


def _expanders():
    heads = jnp.arange(LANES)[:, None]
    e64 = (heads == (jnp.arange(SSM_WIDTH)[None, :] // SSM_HEAD_DIM)).astype(BF16)
    e128 = (heads == (jnp.arange(SSM_HEADS * LANES)[None, :] // LANES)).astype(BF16)
    return e64, e128


def _pad_heads(v):
    return jnp.pad(v, (0, LANES - SSM_HEADS)).reshape(1, LANES)


def _ssd_prompt(xbc, z, dt, lw, batch, seq):
    nc = seq // SSD_CHUNK
    t = SSD_CHUNK
    e64, e128 = _expanders()
    row = lambda width: pl.BlockSpec((t, width), lambda b, c: (b * nc + c, 0))
    const = lambda shape: pl.BlockSpec(shape, lambda b, c: (0,) * len(shape))
    return pl.pallas_call(
        _ssd_prompt_body,
        grid=(batch, nc),
        in_specs=[row(CONV_DIM), row(SSM_WIDTH), row(DT_PAD),
                  const((CONV_W, CONV_DIM)), const((1, CONV_DIM)), const((1, LANES)),
                  const((LANES, LANES)), const((1, SSM_WIDTH)), const((1, SSM_WIDTH)),
                  const((LANES, SSM_WIDTH)), const((LANES, SSM_HEADS * LANES))],
        out_specs=[row(SSM_WIDTH),
                   pl.BlockSpec((1, SSM_HEADS, SSM_HEAD_DIM, D_STATE), lambda b, c: (b, 0, 0, 0))],
        out_shape=[jax.ShapeDtypeStruct((batch * seq, SSM_WIDTH), BF16),
                   jax.ShapeDtypeStruct((batch, SSM_HEADS, SSM_HEAD_DIM, D_STATE), F32)],
        scratch_shapes=[pltpu.VMEM((SUBLANES + t, CONV_DIM), F32),
                        pltpu.VMEM((SSM_HEADS // 2, D_STATE, 2 * SSM_HEAD_DIM), F32)],
        compiler_params=_params("parallel", "arbitrary"),
        name="ssd_prompt",
    )(xbc, z, dt, lw["conv_w"], lw["conv_b"], lw["dt_bias"], lw["a_log_col"], lw["d_skip_e"],
      lw["norm_ssm_out"], e64, e128)


def _ssd_sample_pre_body(xbc_ref, cs_ref, dt_ref, cw_ref, cb_ref, dtb_ref, alog_ref, e64_ref,
                         xs_ref, xdt_t_ref, dec_t_ref, b_ref, c_ref, cs_out_ref):
    rows = [cs_ref[0, w] for w in range(CONV_W - 1)] + [xbc_ref[...]]
    conv = cb_ref[...]
    for w in range(CONV_W):
        conv = conv + rows[w] * cw_ref[w:w + 1, :]
    for w in range(CONV_W - 1):
        cs_out_ref[w] = rows[w + 1]
    xbc = _silu(conv)
    xs = xbc[:, :SSM_WIDTH]
    dt = _softplus(dt_ref[...] + dtb_ref[...])
    dec = jnp.exp(dt * (-jnp.exp(alog_ref[...])))
    dt_e = _dot3_right(dt, e64_ref[...])
    xs_ref[...] = xs
    xdt_t_ref[...] = (xs * dt_e).T
    dec_t_ref[...] = dec.T
    b_ref[...] = xbc[:, SSM_WIDTH:SSM_WIDTH + BC_WIDTH]
    c_ref[...] = xbc[:, SSM_WIDTH + BC_WIDTH:]


def _ssd_sample_pre(layer, xbc, conv_state_t, dt, lw):
    n = xbc.shape[0]
    e64, _ = _expanders()
    sds = lambda *shape: jax.ShapeDtypeStruct(shape, F32)
    full = lambda *shape: pl.BlockSpec(shape, lambda i: (0,) * len(shape))
    return pl.pallas_call(
        _ssd_sample_pre_body,
        grid=(1,),
        in_specs=[full(n, CONV_DIM),
                  pl.BlockSpec((1, CONV_W - 1, n, CONV_DIM), lambda i: (layer, 0, 0, 0)),
                  full(n, DT_PAD), full(CONV_W, CONV_DIM), full(1, CONV_DIM), full(1, LANES),
                  full(1, LANES), full(LANES, SSM_WIDTH)],
        out_specs=[full(n, SSM_WIDTH), full(SSM_WIDTH, n), full(LANES, n), full(n, BC_WIDTH),
                   full(n, BC_WIDTH), full(CONV_W - 1, n, CONV_DIM)],
        out_shape=[sds(n, SSM_WIDTH), sds(SSM_WIDTH, n), sds(LANES, n), sds(n, BC_WIDTH),
                   sds(n, BC_WIDTH), sds(CONV_W - 1, n, CONV_DIM)],
        compiler_params=_params("arbitrary"),
        name="ssd_sample_pre",
    )(xbc, conv_state_t, dt, lw["conv_w"], lw["conv_b"], lw["dt_bias"], lw["a_log"], e64)


def _ssd_sample_state_body(xdt_t_ref, dec_t_ref, b_ref, c_ref, h_ref, hout_ref, y_ref):
    b = pl.program_id(0)
    n = xdt_t_ref.shape[1]
    pick = (lax.broadcasted_iota(jnp.int32, (n, LANES), 0) == b).astype(BF16)
    xdt_col = _dot3_right(xdt_t_ref[...], pick)
    dec_col = _dot3_right(dec_t_ref[...], pick)
    b_row = b_ref[0]
    c_row = c_ref[0]
    grp_id = lax.broadcasted_iota(jnp.int32, (SUBLANES, D_STATE), 0)
    c_rows = jnp.zeros((SUBLANES, D_STATE), F32)
    for grp in range(SSM_GROUPS):
        c_rows = jnp.where(grp_id == grp, c_row[:, grp * D_STATE:(grp + 1) * D_STATE], c_rows)
    for h in range(SSM_HEADS):
        grp = h // HEADS_PER_GROUP
        hs = slice(h * SSM_HEAD_DIM, (h + 1) * SSM_HEAD_DIM)
        hout_ref[0, h] = (dec_col[h:h + 1, :] * h_ref[0, 0, h]
                          + xdt_col[hs, :] * b_row[:, grp * D_STATE:(grp + 1) * D_STATE])
    h_all = hout_ref[0].reshape(SSM_WIDTH, D_STATE).astype(BF16)
    y_all = lax.dot_general(c_rows.astype(BF16), h_all, _NT, preferred_element_type=F32)
    row_grp = lax.broadcasted_iota(jnp.int32, (SUBLANES, SSM_WIDTH), 0)
    col_grp = lax.broadcasted_iota(jnp.int32, (SUBLANES, SSM_WIDTH), 1) // (SSM_WIDTH // SSM_GROUPS)
    y_ref[0] = jnp.sum(jnp.where(row_grp == col_grp, y_all, 0.0), axis=0, keepdims=True)


def _ssd_sample_state(layer, xdt_t, dec_t, bmat, cmat, state):
    n = state.shape[1]
    const = lambda shape: pl.BlockSpec(shape, lambda b: (0,) * len(shape))
    row = lambda width: pl.BlockSpec((1, 1, width), lambda b: (b, 0, 0))
    st_in = pl.BlockSpec((1, 1, SSM_HEADS, SSM_HEAD_DIM, D_STATE), lambda b: (layer, b, 0, 0, 0))
    st = pl.BlockSpec((1, SSM_HEADS, SSM_HEAD_DIM, D_STATE), lambda b: (b, 0, 0, 0))
    return pl.pallas_call(
        _ssd_sample_state_body,
        grid=(n,),
        in_specs=[const((SSM_WIDTH, n)), const((LANES, n)), row(BC_WIDTH), row(BC_WIDTH), st_in],
        out_specs=[st, row(SSM_WIDTH)],
        out_shape=[jax.ShapeDtypeStruct(state.shape[1:], F32),
                   jax.ShapeDtypeStruct((n, 1, SSM_WIDTH), F32)],
        compiler_params=_params("parallel"),
        name="ssd_sample_state",
    )(xdt_t, dec_t, bmat.reshape(n, 1, BC_WIDTH), cmat.reshape(n, 1, BC_WIDTH), state)


def _ssd_sample_post_body(y_ref, xs_ref, z_ref, dskip_ref, gn_ref, o_ref):
    y = y_ref[...] + xs_ref[...] * dskip_ref[...]
    o_ref[...] = _gated_norm(y, z_ref[...], gn_ref[...]).astype(o_ref.dtype)


def _ssd_sample_post(y, xs, z, lw):
    return pl.pallas_call(
        _ssd_sample_post_body,
        out_shape=jax.ShapeDtypeStruct(y.shape, BF16),
        compiler_params=pltpu.CompilerParams(vmem_limit_bytes=VMEM_LIMIT),
        name="ssd_sample_post",
    )(y, xs, z, lw["d_skip_e"], lw["norm_ssm_out"])


def _xattn_heads(q, mk, mv):
    outs = []
    for h in range(X_HEADS):
        hs = slice(h * X_HEAD_DIM, (h + 1) * X_HEAD_DIM)
        s = lax.dot_general((q[:, hs] * (X_HEAD_DIM ** -0.5)).astype(BF16), mk[:, hs], _NT,
                            preferred_element_type=F32)
        s = s - jnp.max(s, axis=1, keepdims=True)
        p = jnp.exp(s)
        p = p / jnp.sum(p, axis=1, keepdims=True)
        outs.append(jnp.dot(p.astype(BF16), mv[:, hs], preferred_element_type=F32).astype(BF16))
    return jnp.concatenate(outs, axis=1)


def _mix_xattn_prompt_body(x_ref, att_ref, y_ref, wa_ref, wy_ref, g_ref, wq_ref, mk_ref, mv_ref,
                           wo_ref, o_ref):
    x = x_ref[...] + jnp.dot(att_ref[...], wa_ref[...], preferred_element_type=F32) \
        + jnp.dot(y_ref[...], wy_ref[...], preferred_element_type=F32)
    u = _rms(x, g_ref[...]).astype(BF16)
    q = jnp.dot(u, wq_ref[...], preferred_element_type=F32)
    o = _xattn_heads(q, mk_ref[0].astype(BF16), mv_ref[0].astype(BF16))
    o_ref[...] = x + jnp.dot(o, wo_ref[...], preferred_element_type=F32)


def _mix_xattn_prompt(x, att, y, lw, mem_k, mem_v, batch, seq):
    tm = 256
    nt = seq // tm
    n_mem = mem_k.shape[1]
    row = lambda width: pl.BlockSpec((tm, width), lambda i: (i, 0))
    const = lambda shape: pl.BlockSpec(shape, lambda i: (0,) * len(shape))
    mem = pl.BlockSpec((1, n_mem, D_MODEL), lambda i: (i // nt, 0, 0))
    return pl.pallas_call(
        _mix_xattn_prompt_body,
        grid=(batch * nt,),
        in_specs=[row(D_MODEL), row(ATT_WIDTH), row(SSM_WIDTH),
                  const((ATT_WIDTH, D_MODEL)), const((SSM_WIDTH, D_MODEL)), const((1, D_MODEL)),
                  const((D_MODEL, D_MODEL)), mem, mem, const((D_MODEL, D_MODEL))],
        out_specs=row(D_MODEL),
        out_shape=jax.ShapeDtypeStruct(x.shape, F32),
        compiler_params=_params("parallel"),
        name="mix_xattn_prompt",
    )(x, att, y, lw["w_mix_att"], lw["w_mix_ssm"], lw["norm_xattn"], lw["w_xq"], mem_k, mem_v,
      lw["w_xo"])


def _mem_kv_body(m_ref, wk_ref, wv_ref, k_ref, v_ref):
    m = m_ref[...].astype(BF16)
    k_ref[...] = jnp.dot(m, wk_ref[...], preferred_element_type=F32)
    v_ref[...] = jnp.dot(m, wv_ref[...], preferred_element_type=F32)


def _mem_kv(mem, wk, wv):
    n = mem.shape[0]
    tm = min(256, n)
    row = pl.BlockSpec((tm, D_MODEL), lambda i: (i, 0))
    const = pl.BlockSpec((D_MODEL, D_MODEL), lambda i: (0, 0))
    return pl.pallas_call(
        _mem_kv_body,
        grid=(n // tm,),
        in_specs=[row, const, const],
        out_specs=[row, row],
        out_shape=[jax.ShapeDtypeStruct((n, D_MODEL), F32)] * 2,
        compiler_params=_params("parallel"),
        name="mem_kv",
    )(mem, wk, wv)


def _mix_sample_body(x_ref, att_t_ref, y_ref, wa_ref, wy_ref, g_ref, wq_ref, x1_ref, q_ref):
    att = att_t_ref[...].T.astype(BF16)
    x = x_ref[...] + jnp.dot(att, wa_ref[...], preferred_element_type=F32) \
        + jnp.dot(y_ref[...], wy_ref[...], preferred_element_type=F32)
    x1_ref[...] = x
    u = _rms(x, g_ref[...]).astype(BF16)
    q_ref[...] = jnp.dot(u, wq_ref[...], preferred_element_type=F32)


def _mix_sample(x, att, y, lw):
    return pl.pallas_call(
        _mix_sample_body,
        out_shape=[jax.ShapeDtypeStruct(x.shape, F32)] * 2,
        compiler_params=pltpu.CompilerParams(vmem_limit_bytes=VMEM_LIMIT),
        name="mix_sample",
    )(x, att, y, lw["w_mix_att"], lw["w_mix_ssm"], lw["norm_xattn"], lw["w_xq"])


def _xattn_sample_body(q_ref, mk_ref, mv_ref, o_ref):
    q = q_ref[0] * (X_HEAD_DIM ** -0.5)
    s = jnp.sum(mk_ref[0, 0] * q[None], axis=-1, keepdims=True)
    p = jnp.exp(s - jnp.max(s, axis=0, keepdims=True))
    denom = jnp.sum(p, axis=0)
    o_ref[0] = jnp.sum(p * mv_ref[0, 0], axis=0) / denom


def _xattn_sample(layer, q, mem_k, mem_v):
    n, n_mem = mem_k.shape[1:3]
    row = pl.BlockSpec((1, X_HEADS, X_HEAD_DIM), lambda b: (b, 0, 0))
    mem = pl.BlockSpec((1, 1, n_mem, X_HEADS, X_HEAD_DIM), lambda b: (layer, b, 0, 0, 0))
    return pl.pallas_call(
        _xattn_sample_body,
        grid=(n,),
        in_specs=[row, mem, mem],
        out_specs=row,
        out_shape=jax.ShapeDtypeStruct((n, X_HEADS, X_HEAD_DIM), F32),
        compiler_params=_params("parallel"),
        name="xattn_sample",
    )(q.reshape(n, X_HEADS, X_HEAD_DIM), mem_k, mem_v).reshape(n, D_MODEL)


def _proj_residual_body(x_ref, a_ref, w_ref, o_ref):
    o_ref[...] = x_ref[...] + jnp.dot(a_ref[...].astype(BF16), w_ref[...],
                                      preferred_element_type=F32)


def _proj_residual(x, a, w):
    return pl.pallas_call(
        _proj_residual_body,
        out_shape=jax.ShapeDtypeStruct(x.shape, F32),
        compiler_params=pltpu.CompilerParams(vmem_limit_bytes=VMEM_LIMIT),
        name="proj_residual",
    )(x, a, w)


def _rope_tables(pos):
    half = ROT_DIM // 2
    inv_freq = ROPE_THETA ** (-(jnp.arange(0, ROT_DIM, 2, dtype=F32) / ROT_DIM))
    ang = pos.astype(F32)[:, None] * inv_freq[None, :]
    cos, sin = jnp.cos(ang), jnp.sin(ang)
    n = pos.shape[0]
    zeros = lambda w: jnp.zeros((n, w), F32)
    cos_h = jnp.concatenate([cos, cos, jnp.ones((n, HEAD_DIM - ROT_DIM), F32)], axis=1)
    sin_hi = jnp.concatenate([-sin, zeros(HEAD_DIM - half)], axis=1)
    sin_lo = jnp.concatenate([zeros(half), sin, zeros(HEAD_DIM - ROT_DIM)], axis=1)
    two = lambda tbl: jnp.concatenate([tbl, tbl], axis=1)
    return two(cos_h), two(sin_hi), two(sin_lo)


def _layer_weights(l, p):
    row = lambda v: v[l].reshape(1, -1)
    w_mix_in = jnp.pad(p["w_mix_in"][l], ((0, 0), (0, DT_PAD - SSM_HEADS))).astype(BF16)
    w_mix_out = p["w_mix_out"][l].astype(BF16)
    return {
        "norm_ffn1": row(p["norm_ffn1"]), "ffn1_w_in": p["ffn1_w_in"][l].astype(BF16),
        "ffn1_w_out": p["ffn1_w_out"][l].astype(BF16),
        "norm_mix": row(p["norm_mix"]), "w_mix_in": w_mix_in,
        "conv_w": p["conv_w"][l], "conv_b": row(p["conv_b"]),
        "dt_bias": _pad_heads(p["dt_bias"][l]), "a_log": _pad_heads(p["a_log"][l]),
        "a_log_col": jnp.broadcast_to(_pad_heads(p["a_log"][l]).reshape(LANES, 1), (LANES, LANES)),
        "d_skip_e": jnp.repeat(p["d_skip"][l], SSM_HEAD_DIM).reshape(1, SSM_WIDTH),
        "norm_ssm_out": row(p["norm_ssm_out"]),
        "w_mix_att": w_mix_out[:ATT_WIDTH], "w_mix_ssm": w_mix_out[ATT_WIDTH:],
        "norm_xattn": row(p["norm_xattn"]), "w_xq": p["w_xq"][l].astype(BF16),
        "w_xk": p["w_xk"][l].astype(BF16), "w_xv": p["w_xv"][l].astype(BF16),
        "w_xo": p["w_xo"][l].astype(BF16),
        "norm_ffn2": row(p["norm_ffn2"]), "ffn2_w_in": p["ffn2_w_in"][l].astype(BF16),
        "ffn2_w_out": p["ffn2_w_out"][l].astype(BF16),
    }


def kernel(x_prompt, x_sample, mem_prompt, cache_k, cache_v, page_table, state_ssm, state_conv,
           cache_mem_k, cache_mem_v, norm_ffn1, ffn1_w_in, ffn1_w_out, norm_mix, w_mix_in, conv_w,
           conv_b, dt_bias, a_log, d_skip, norm_ssm_out, w_mix_out, norm_xattn, w_xq, w_xk, w_xv,
           w_xo, norm_ffn2, ffn2_w_in, ffn2_w_out, norm_final):
    params = dict(norm_ffn1=norm_ffn1, ffn1_w_in=ffn1_w_in, ffn1_w_out=ffn1_w_out, norm_mix=norm_mix,
                  w_mix_in=w_mix_in, conv_w=conv_w, conv_b=conv_b, dt_bias=dt_bias, a_log=a_log,
                  d_skip=d_skip, norm_ssm_out=norm_ssm_out, w_mix_out=w_mix_out,
                  norm_xattn=norm_xattn, w_xq=w_xq, w_xk=w_xk, w_xv=w_xv, w_xo=w_xo,
                  norm_ffn2=norm_ffn2, ffn2_w_in=ffn2_w_in, ffn2_w_out=ffn2_w_out)
    depth = norm_ffn1.shape[0]
    b_p, s_p, _ = x_prompt.shape
    b_s, s_s, _ = x_sample.shape
    n_mem = mem_prompt.shape[1]
    n_phys, page = cache_k.shape[1:3]
    past_len = page_table.shape[1] * page
    assert s_s == 1 and s_p % MOBA_BLOCK == 0 and past_len % MOBA_BLOCK == 0
    assert (b_p * s_p) % 512 == 0 and b_s % SUBLANES == 0

    tab_p = _rope_tables(jnp.arange(s_p, dtype=jnp.int32))
    tab_s = _rope_tables(jnp.full((b_s,), past_len, jnp.int32))
    g_final = norm_final.reshape(1, D_MODEL)
    mem_flat = mem_prompt.reshape(b_p * n_mem, D_MODEL)
    cache_k_t = jnp.transpose(cache_k, (0, 1, 3, 4, 2))
    cache_v_t = jnp.transpose(cache_v, (0, 1, 3, 4, 2))
    conv_state_t = jnp.transpose(state_conv, (0, 2, 1, 3))

    xp = x_prompt.reshape(b_p * s_p, D_MODEL)
    xs = x_sample.reshape(b_s, D_MODEL)
    outs = {k: [] for k in ("kp", "vp", "ks", "vs", "hp", "hs", "cp", "cs", "mk", "mv")}
    for l in range(depth):
        lw = _layer_weights(l, params)
        last = l == depth - 1

        mem_k, mem_v = _mem_kv(mem_flat, lw["w_xk"], lw["w_xv"])
        xp = _ffn(xp, lw["norm_ffn1"], lw["ffn1_w_in"], lw["ffn1_w_out"], g_final, False)
        q, k_t, v_t, kb, vb, z, xbc, dt, ksum = _inproj(xp, lw["norm_mix"], lw["w_mix_in"], *tab_p,
                                                        tm=MOBA_BLOCK, seq_len=s_p)
        kmean = (ksum * (1.0 / MOBA_BLOCK)).reshape(b_p, s_p // MOBA_BLOCK, ATT_WIDTH)
        att = _moba_prompt(q, kb, vb, kmean, b_p, s_p)
        y, h_p = _ssd_prompt(xbc, z, dt, lw, b_p, s_p)
        xp = _mix_xattn_prompt(xp, att, y, lw, mem_k.reshape(b_p, n_mem, D_MODEL),
                               mem_v.reshape(b_p, n_mem, D_MODEL), b_p, s_p)
        xp = _ffn(xp, lw["norm_ffn2"], lw["ffn2_w_in"], lw["ffn2_w_out"], g_final, last)
        outs["kp"].append(k_t)
        outs["vp"].append(v_t)
        outs["hp"].append(h_p)
        outs["cp"].append(xbc.reshape(b_p, s_p, CONV_DIM)[:, s_p - (CONV_W - 1):])
        outs["mk"].append(mem_k.reshape(b_p, n_mem, X_HEADS, X_HEAD_DIM))
        outs["mv"].append(mem_v.reshape(b_p, n_mem, X_HEADS, X_HEAD_DIM))

        xs = _ffn(xs, lw["norm_ffn1"], lw["ffn1_w_in"], lw["ffn1_w_out"], g_final, False)
        q, k_t, v_t, _, _, z, xbc, dt, _ = _inproj(xs, lw["norm_mix"], lw["w_mix_in"], *tab_s,
                                                   tm=b_s, seq_len=b_s)
        att_t = _moba_sample(l, page_table, jnp.concatenate([q.T, k_t[0], v_t[0]], axis=0),
                             cache_k_t, cache_v_t)
        xs_ssm, xdt_t, dec_t, bmat, cmat, conv_new = _ssd_sample_pre(l, xbc, conv_state_t, dt, lw)
        h_s, y = _ssd_sample_state(l, xdt_t, dec_t, bmat, cmat, state_ssm)
        y = _ssd_sample_post(y.reshape(b_s, SSM_WIDTH), xs_ssm, z, lw)
        xs, xq = _mix_sample(xs, att_t, y, lw)
        o = _xattn_sample(l, xq, cache_mem_k, cache_mem_v)
        xs = _proj_residual(xs, o, lw["w_xo"])
        xs = _ffn(xs, lw["norm_ffn2"], lw["ffn2_w_in"], lw["ffn2_w_out"], g_final, last)
        outs["ks"].append(k_t)
        outs["vs"].append(v_t)
        outs["hs"].append(h_s)
        outs["cs"].append(conv_new)

    st = lambda key: jnp.stack(outs[key])

    def kv_out(key, runs, length):
        t = st(key).reshape(depth, runs, ATT_HEADS, HEAD_DIM, length)
        return jnp.transpose(t, (0, 1, 4, 2, 3))

    ks, vs = (jnp.transpose(kv_out(key, 1, b_s), (0, 2, 1, 3, 4)) for key in ("ks", "vs"))
    return (xp.reshape(b_p, s_p, D_MODEL), xs.reshape(b_s, 1, D_MODEL),
            kv_out("kp", b_p, s_p), kv_out("vp", b_p, s_p), ks, vs, st("hp"), st("hs"), st("cp"),
            jnp.transpose(st("cs"), (0, 2, 1, 3)), st("mk"), st("mv"))
```
